```python
import jax, jax.numpy as jnp
from jax import lax
import numpy as np

D_MODEL = 1024
BATCH = 16
SEQ = 2048
DEPTH = 2
DEC_BATCH = 8
DEC_SEQ = 2048
PAST_LEN = 128

GRID_W = 64
N_MIXERS = 2
N_A_LAYERS = (DEPTH + 1) // 2
N_B_LAYERS = DEPTH // 2
NA_HEADS = 16
NA_HEAD_DIM = D_MODEL // NA_HEADS
WIN_R = 8
WIN_C = 16
N_COL_BLK = GRID_W // WIN_C
KEY_COLS = 2 * WIN_C
FN_GROUPS = 4
FN_GROUP_DIM = D_MODEL // FN_GROUPS
N_EXPERTS = 16
EC_FACTOR = 2
EXPERT_FF = 2048
N_MOD = 6
EPS = 1e-6
NEG = -1e30

kernel_name = 'hybrid_natten_fnet_ec_moe_encoder'


def _rms(x, g):
    xf = x.astype(jnp.float32)
    y = xf * lax.rsqrt(jnp.mean(xf * xf, axis=-1, keepdims=True) + EPS)
    return (y * g.astype(jnp.float32)).astype(x.dtype)


def _col_tables():
    j = np.arange(N_COL_BLK)
    kc0 = np.clip(j * WIN_C - WIN_C // 2, 0, GRID_W - KEY_COLS)
    key_col = kc0[:, None] + np.arange(KEY_COLS)[None, :]
    q_col = j[:, None] * WIN_C + np.arange(WIN_C)[None, :]
    start = np.clip(q_col - WIN_C // 2, 0, GRID_W - WIN_C)
    rel = key_col[:, None, :] - start[:, :, None]
    mask = (rel >= 0) & (rel < WIN_C)
    dcol = np.clip(key_col[:, None, :] - q_col[:, :, None] + WIN_C - 1, 0, 2 * WIN_C - 2)
    return key_col, mask, dcol


def neighbourhood_attention(h, w_qkv, q_g, k_g, rpb, w_o):
    B, T, D = h.shape
    rows = T // GRID_W
    kr = min(WIN_R, rows)
    q, k, v = jnp.split(h @ w_qkv, 3, axis=-1)
    shp = (B, rows, GRID_W, NA_HEADS, NA_HEAD_DIM)
    q = _rms(q.reshape(shp), q_g) * (NA_HEAD_DIM ** -0.5)
    k = _rms(k.reshape(shp), k_g)
    v = v.reshape(shp)
    key_col, mask_np, dcol = _col_tables()
    mask = jnp.asarray(mask_np)[:, :, None, :]

    def row_block(r):
        rs = jnp.clip(r - kr // 2, 0, rows - kr)
        k_rows = lax.dynamic_slice_in_dim(k, rs, kr, axis=1)
        v_rows = lax.dynamic_slice_in_dim(v, rs, kr, axis=1)
        k_blk = k_rows[:, :, key_col]
        v_blk = v_rows[:, :, key_col]
        q_row = lax.dynamic_index_in_dim(q, r, axis=1, keepdims=False)
        q_row = q_row.reshape(B, N_COL_BLK, WIN_C, NA_HEADS, NA_HEAD_DIM)
        s = jnp.einsum('bjqhd,brjkhd->bhjqrk', q_row, k_blk).astype(jnp.float32)
        drow = rs + jnp.arange(kr) - r + WIN_R - 1
        bias = rpb[:, drow[None, None, :, None], dcol[:, :, None, :]]
        s = jnp.where(mask, s + bias.astype(jnp.float32), NEG)
        p = jax.nn.softmax(s.reshape(s.shape[:4] + (kr * KEY_COLS,)), axis=-1)
        p = p.reshape(s.shape).astype(v.dtype)
        o = jnp.einsum('bhjqrk,brjkhd->bjqhd', p, v_blk)
        return o.reshape(B, GRID_W, NA_HEADS, NA_HEAD_DIM)

    o = lax.map(row_block, jnp.arange(rows))
    o = jnp.moveaxis(o, 0, 1).reshape(B, T, D)
    return o @ w_o


def fourier_mixer(h, w_in, w_out):
    B, T, D = h.shape
    u = (h @ w_in).astype(jnp.float32).reshape(B, T, FN_GROUPS, FN_GROUP_DIM)
    f = jnp.fft.fft2(u, axes=(1, 3), norm='ortho').real
    return f.reshape(B, T, D).astype(h.dtype) @ w_out


def expert_choice_ffn(h, w_router, w_gate, w_up, w_down):
    B, T, D = h.shape
    n = B * T
    cap = max(1, EC_FACTOR * n // N_EXPERTS)
    xt = h.reshape(n, D)
    aff = jax.nn.softmax((xt @ w_router).astype(jnp.float32), axis=-1)
    g, idx = lax.top_k(aff.T, cap)
    xe = xt[idx]
    a = jnp.einsum('ecd,edf->ecf', xe, w_gate)
    b = jnp.einsum('ecd,edf->ecf', xe, w_up)
    ye = jnp.einsum('ecf,efd->ecd', jax.nn.silu(a) * b, w_down)
    ye = ye * g[..., None].astype(ye.dtype)
    out = jnp.zeros_like(xt).at[idx.reshape(-1)].add(ye.reshape(-1, D))
    return out.reshape(B, T, D)


def trunk(x, c, norm1_g, norm2_g, ada_w, ada_b, na_w_qkv, na_q_g, na_k_g, na_rpb, na_w_o,
          fn_w_in, fn_w_out, moe_w_router, moe_w_gate, moe_w_up, moe_w_down):
    for i in range(DEPTH):
        mod = jax.nn.silu(c) @ ada_w[i] + ada_b[i]
        sh1, sc1, g1, sh2, sc2, g2 = [m[:, None, :] for m in jnp.split(mod, N_MOD, axis=-1)]
        h = _rms(x, norm1_g[i]) * (1 + sc1) + sh1
        j = i // N_MIXERS
        if i % N_MIXERS == 0:
            m = neighbourhood_attention(h, na_w_qkv[j], na_q_g[j], na_k_g[j], na_rpb[j], na_w_o[j])
        else:
            m = fourier_mixer(h, fn_w_in[j], fn_w_out[j])
        x = x + g1 * m
        h = _rms(x, norm2_g[i]) * (1 + sc2) + sh2
        x = x + g2 * expert_choice_ffn(h, moe_w_router[i], moe_w_gate[i], moe_w_up[i], moe_w_down[i])
    return x


def setup_inputs(seed: int = 0) -> dict:
    key = jax.random.key(seed)
    ks = jax.random.split(key, 20)
    D, F, E, H = D_MODEL, EXPERT_FF, N_EXPERTS, NA_HEADS
    nrm = lambda k, shp, s: jax.random.normal(k, shp, jnp.float32) * s
    return {
        'x_prompt': nrm(ks[0], (BATCH, SEQ, D), 1.0),
        'x_sample': nrm(ks[1], (DEC_BATCH, DEC_SEQ, D), 1.0),
        'c_prompt': nrm(ks[2], (BATCH, D), 1.0),
        'c_sample': nrm(ks[3], (DEC_BATCH, D), 1.0),
        'norm1_g': 1.0 + nrm(ks[4], (DEPTH, D), 0.02),
        'norm2_g': 1.0 + nrm(ks[5], (DEPTH, D), 0.02),
        'ada_w': nrm(ks[6], (DEPTH, D, N_MOD * D), 0.5 * D ** -0.5),
        'ada_b': nrm(ks[7], (DEPTH, N_MOD * D), 0.02),
        'na_w_qkv': nrm(ks[8], (N_A_LAYERS, D, 3 * D), D ** -0.5),
        'na_q_g': 1.0 + nrm(ks[9], (N_A_LAYERS, NA_HEAD_DIM), 0.02),
        'na_k_g': 1.0 + nrm(ks[10], (N_A_LAYERS, NA_HEAD_DIM), 0.02),
        'na_rpb': nrm(ks[11], (N_A_LAYERS, H, 2 * WIN_R - 1, 2 * WIN_C - 1), 0.1),
        'na_w_o': nrm(ks[12], (N_A_LAYERS, D, D), D ** -0.5),
        'fn_w_in': nrm(ks[13], (N_B_LAYERS, D, D), D ** -0.5),
        'fn_w_out': nrm(ks[14], (N_B_LAYERS, D, D), D ** -0.5),
        'moe_w_router': nrm(ks[15], (DEPTH, D, E), D ** -0.5),
        'moe_w_gate': nrm(ks[16], (DEPTH, E, D, F), D ** -0.5),
        'moe_w_up': nrm(ks[17], (DEPTH, E, D, F), D ** -0.5),
        'moe_w_down': nrm(ks[18], (DEPTH, E, F, D), F ** -0.5),
    }


def reference(x_prompt, x_sample, c_prompt, c_sample, norm1_g, norm2_g, ada_w, ada_b,
              na_w_qkv, na_q_g, na_k_g, na_rpb, na_w_o, fn_w_in, fn_w_out,
              moe_w_router, moe_w_gate, moe_w_up, moe_w_down):
    y_prompt = trunk(x_prompt, c_prompt, norm1_g, norm2_g, ada_w, ada_b, na_w_qkv, na_q_g, na_k_g,
                     na_rpb, na_w_o, fn_w_in, fn_w_out, moe_w_router, moe_w_gate, moe_w_up, moe_w_down)
    y_sample = trunk(x_sample, c_sample, norm1_g, norm2_g, ada_w, ada_b, na_w_qkv, na_q_g, na_k_g,
                     na_rpb, na_w_o, fn_w_in, fn_w_out, moe_w_router, moe_w_gate, moe_w_up, moe_w_down)
    return (y_prompt, y_sample)
```

```python
import functools

import numpy as np
import jax
import jax.numpy as jnp
from jax import lax
from jax.experimental import pallas as pl
from jax.experimental.pallas import tpu as pltpu

D_MODEL = 1024
SEQ = 2048
GRID_W = 64
ROWS = SEQ // GRID_W
NA_HEADS = 16
HEAD_DIM = D_MODEL // NA_HEADS
WIN_R = 8
WIN_C = 16
FN_GROUPS = 4
FN_GROUP_DIM = D_MODEL // FN_GROUPS
N_EXPERTS = 16
EC_FACTOR = 2
EXPERT_FF = 2048
N_MOD = 6
EPS = 1e-6
NEG = -1e30

LANES = 128
KEY_ROWS = min(WIN_R, ROWS)
KEYS = KEY_ROWS * GRID_W
HEADS_PER_BLOCK = LANES // HEAD_DIM
N_ROW_OFFSETS = KEY_ROWS

VMEM_LIMIT = 52 * 1024 * 1024

F32 = jnp.float32
BF16 = jnp.bfloat16


def _params(sem):
    return pltpu.CompilerParams(dimension_semantics=sem, vmem_limit_bytes=VMEM_LIMIT)


def _dot(a, b):
    return jnp.dot(a, b, preferred_element_type=F32)


def _dot_nt(a, b):
    return lax.dot_general(a, b, (((1,), (1,)), ((), ())), preferred_element_type=F32)


def _dot_split(a, b):
    a_hi = a.astype(BF16)
    a_lo = (a - a_hi.astype(F32)).astype(BF16)
    b_hi = b.astype(BF16)
    b_lo = (b - b_hi.astype(F32)).astype(BF16)
    return _dot(a_hi, b_hi) + (_dot(a_hi, b_lo) + _dot(a_lo, b_hi))


def _silu(a):
    return a * (1.0 / (1.0 + jnp.exp(-a)))


def _rms_mod(x, gain, shift, scale):
    ms = jnp.mean(x * x, axis=-1, keepdims=True)
    y = x * lax.rsqrt(ms + EPS) * gain
    return y * (1.0 + scale) + shift


def _ada_kernel(c_ref, w_ref, b_ref, o_ref):
    o_ref[...] = _dot_split(_silu(c_ref[...]), w_ref[...]) + b_ref[...]


def _ada(c, ada_w, ada_b):
    depth, d, n6 = ada_w.shape
    b = c.shape[0]
    tn = 1536
    out = pl.pallas_call(
        _ada_kernel,
        grid=(depth, n6 // tn),
        in_specs=[
            pl.BlockSpec((b, d), lambda i, j: (0, 0)),
            pl.BlockSpec((None, d, tn), lambda i, j: (i, 0, j)),
            pl.BlockSpec((None, 1, tn), lambda i, j: (i, 0, j)),
        ],
        out_specs=pl.BlockSpec((None, b, tn), lambda i, j: (i, 0, j)),
        out_shape=jax.ShapeDtypeStruct((depth, b, n6), F32),
        compiler_params=_params(("arbitrary", "arbitrary")),
        name="ada_mod",
    )(c, ada_w, ada_b.reshape(depth, 1, n6))
    return out.reshape(depth, b, N_MOD, d)


ROW_TILE = 512


def _qkv_kernel(x_ref, mod_ref, g_ref, w_ref, o_ref):
    h = _rms_mod(x_ref[...], g_ref[...], mod_ref[0:1, :], mod_ref[1:2, :])
    o_ref[...] = _dot(h.astype(BF16), w_ref[...]).astype(BF16)


def _norm_qkv(x, mod, gain, w):
    n, d = x.shape
    nn = w.shape[1]
    tiles_per_seq = SEQ // ROW_TILE
    return pl.pallas_call(
        _qkv_kernel,
        grid=(n // ROW_TILE,),
        in_specs=[
            pl.BlockSpec((ROW_TILE, d), lambda i: (i, 0)),
            pl.BlockSpec((None, N_MOD, d), lambda i: (i // tiles_per_seq, 0, 0)),
            pl.BlockSpec((1, d), lambda i: (0, 0)),
            pl.BlockSpec((d, nn), lambda i: (0, 0)),
        ],
        out_specs=pl.BlockSpec((ROW_TILE, nn), lambda i: (i, 0)),
        out_shape=jax.ShapeDtypeStruct((n, nn), BF16),
        compiler_params=_params(("arbitrary",)),
        name="norm_qkv",
    )(x, mod, gain, w)


def _fnet_in_kernel(x_ref, mod_ref, g_ref, w_ref, cs_ref, o_ref):
    h = _rms_mod(x_ref[...], g_ref[...], mod_ref[0:1, :], mod_ref[1:2, :])
    u = _dot(h.astype(BF16), w_ref[...]).astype(BF16)
    gd = FN_GROUP_DIM
    for g in range(FN_GROUPS):
        v = _dot(u[:, g * gd:(g + 1) * gd], cs_ref[...])
        o_ref[:, g * gd:(g + 1) * gd] = v[:, :gd].astype(BF16)
        o_ref[:, D_MODEL + g * gd:D_MODEL + (g + 1) * gd] = v[:, gd:].astype(BF16)


def _fnet_in(x, mod, gain, w, cs):
    n, d = x.shape
    tiles_per_seq = SEQ // ROW_TILE
    return pl.pallas_call(
        _fnet_in_kernel,
        grid=(n // ROW_TILE,),
        in_specs=[
            pl.BlockSpec((ROW_TILE, d), lambda i: (i, 0)),
            pl.BlockSpec((None, N_MOD, d), lambda i: (i // tiles_per_seq, 0, 0)),
            pl.BlockSpec((1, d), lambda i: (0, 0)),
            pl.BlockSpec((d, d), lambda i: (0, 0)),
            pl.BlockSpec(cs.shape, lambda i: (0, 0)),
        ],
        out_specs=pl.BlockSpec((ROW_TILE, 2 * d), lambda i: (i, 0)),
        out_shape=jax.ShapeDtypeStruct((n, 2 * d), BF16),
        compiler_params=_params(("arbitrary",)),
        name="fnet_in",
    )(x, mod, gain, w, cs)


SEQ_DFT_TILE = 1024
SEQ_DFT_SUB = 256


def _seq_dft_kernel(ct_ref, st_ref, v_ref, o_ref):
    scale = 1.0 / np.sqrt(SEQ)
    for s in range(SEQ_DFT_TILE // SEQ_DFT_SUB):
        rows = slice(s * SEQ_DFT_SUB, (s + 1) * SEQ_DFT_SUB)
        acc = _dot(ct_ref[rows, :], v_ref[:, :D_MODEL]) + _dot(st_ref[rows, :], v_ref[:, D_MODEL:])
        o_ref[rows, :] = (acc * scale).astype(BF16)


def _seq_dft(v, ct, st_neg):
    b = v.shape[0]
    return pl.pallas_call(
        _seq_dft_kernel,
        grid=(SEQ // SEQ_DFT_TILE, b),
        in_specs=[
            pl.BlockSpec((SEQ_DFT_TILE, SEQ), lambda k, i: (k, 0)),
            pl.BlockSpec((SEQ_DFT_TILE, SEQ), lambda k, i: (k, 0)),
            pl.BlockSpec((None, SEQ, 2 * D_MODEL), lambda k, i: (i, 0, 0)),
        ],
        out_specs=pl.BlockSpec((None, SEQ_DFT_TILE, D_MODEL), lambda k, i: (i, k, 0)),
        out_shape=jax.ShapeDtypeStruct((b, SEQ, D_MODEL), BF16),
        compiler_params=_params(("arbitrary", "arbitrary")),
        name="seq_dft",
    )(ct, st_neg, v)


def _dft_tables():
    def cos_sin(n):
        k = lax.broadcasted_iota(jnp.int32, (n, n), 0)
        t = lax.broadcasted_iota(jnp.int32, (n, n), 1)
        ang = ((k * t) % n).astype(F32) * (2.0 * np.pi / n)
        return jnp.cos(ang), jnp.sin(ang)
    cc, sc = cos_sin(FN_GROUP_DIM)
    cs = (jnp.concatenate([cc, sc], axis=1) * (1.0 / np.sqrt(FN_GROUP_DIM))).astype(BF16)
    ct, st = cos_sin(SEQ)
    return cs, ct.astype(BF16), (-st).astype(BF16)


def _attn_bias_tables(rpb):
    q = np.arange(GRID_W)
    kc = np.arange(GRID_W)
    start = np.clip(q - WIN_C // 2, 0, GRID_W - WIN_C)
    rel = kc[None, :] - start[:, None]
    mask = (rel >= 0) & (rel < WIN_C)
    dcol = np.clip(kc[None, :] - q[:, None] + WIN_C - 1, 0, 2 * WIN_C - 2)
    o = np.arange(N_ROW_OFFSETS)
    i = np.arange(KEY_ROWS)
    drow = i[None, :] - o[:, None] + WIN_R - 1
    tbl = rpb[:, drow[:, None, :, None], dcol[None, :, None, :]]
    tbl = jnp.where(jnp.asarray(mask)[None, None, :, None, :], tbl.astype(F32), NEG)
    tbl = tbl.reshape(NA_HEADS // HEADS_PER_BLOCK, HEADS_PER_BLOCK, N_ROW_OFFSETS, GRID_W, KEYS)
    return tbl


def _attn_kernel(q_ref, k_ref, v_ref, qg_ref, kg_ref, bias_ref, o_ref, qn_ref, kn_ref):
    lane = lax.broadcasted_iota(jnp.int32, (1, LANES), 1)
    lo = lane < HEAD_DIM

    def head_norm(x, g):
        sq = x * x
        s_lo = jnp.sum(jnp.where(lo, sq, 0.0), axis=-1, keepdims=True)
        s_hi = jnp.sum(jnp.where(lo, 0.0, sq), axis=-1, keepdims=True)
        ms = jnp.where(lo, s_lo, s_hi) * (1.0 / HEAD_DIM)
        return x * lax.rsqrt(ms + EPS) * g

    qn_ref[...] = head_norm(q_ref[...].astype(F32), qg_ref[...]).astype(BF16)
    kn_ref[...] = head_norm(k_ref[...].astype(F32), kg_ref[...]).astype(BF16)

    def row(r, carry):
        rs = jnp.clip(r - KEY_ROWS // 2, 0, ROWS - KEY_ROWS)
        off = r - rs
        k0 = pl.multiple_of(rs * GRID_W, GRID_W)
        q0 = pl.multiple_of(r * GRID_W, GRID_W)
        kr = kn_ref[pl.ds(k0, KEYS), :]
        vr = v_ref[pl.ds(k0, KEYS), :]
        qr = qn_ref[pl.ds(q0, GRID_W), :]
        outs = []
        for h in range(HEADS_PER_BLOCK):
            keep = lo if h == 0 else jnp.logical_not(lo)
            qm = jnp.where(keep, qr, jnp.zeros_like(qr))
            s = _dot_nt(qm, kr) + bias_ref[h, off]
            m = jnp.max(s, axis=-1, keepdims=True)
            p = jnp.exp(s - m)
            l = jnp.sum(p, axis=-1, keepdims=True)
            outs.append(_dot(p.astype(BF16), vr) * (1.0 / l))
        o_ref[pl.ds(q0, GRID_W), :] = jnp.where(lo, outs[0], outs[1]).astype(BF16)
        return carry

    lax.fori_loop(0, ROWS, row, 0, unroll=4)


def _attention(qkv, q_gain, k_gain, bias):
    b = qkv.shape[0]
    nblk = D_MODEL // LANES
    blk = lambda base: pl.BlockSpec((None, SEQ, LANES), lambda hp, i: (i, 0, base + hp))
    return pl.pallas_call(
        _attn_kernel,
        grid=(nblk, b),
        in_specs=[
            blk(0), blk(nblk), blk(2 * nblk),
            pl.BlockSpec((1, LANES), lambda hp, i: (0, 0)),
            pl.BlockSpec((1, LANES), lambda hp, i: (0, 0)),
            pl.BlockSpec((None, HEADS_PER_BLOCK, N_ROW_OFFSETS, GRID_W, KEYS),
                         lambda hp, i: (hp, 0, 0, 0, 0)),
        ],
        out_specs=pl.BlockSpec((None, SEQ, LANES), lambda hp, i: (i, 0, hp)),
        out_shape=jax.ShapeDtypeStruct((b, SEQ, D_MODEL), BF16),
        scratch_shapes=[pltpu.VMEM((SEQ, LANES), BF16), pltpu.VMEM((SEQ, LANES), BF16)],
        compiler_params=_params(("arbitrary", "arbitrary")),
        name="nbr_attention",
    )(qkv, qkv, qkv, q_gain, k_gain, bias)


def _mix_out_kernel(m_ref, w_ref, x_ref, mod_ref, g_ref, wr_ref, x1_ref, h2_ref, lg_ref):
    x1 = x_ref[...] + mod_ref[2:3, :] * _dot(m_ref[...], w_ref[...])
    x1_ref[...] = x1
    h2 = _rms_mod(x1, g_ref[...], mod_ref[3:4, :], mod_ref[4:5, :])
    h2_ref[...] = h2.astype(BF16)
    lg_ref[...] = _dot_split(h2, wr_ref[...])


def _mix_out(m, w, x, mod, gain, w_router):
    n, d = x.shape
    tiles_per_seq = SEQ // ROW_TILE
    row = lambda i: (i, 0)
    fixed = lambda i: (0, 0)
    return pl.pallas_call(
        _mix_out_kernel,
        grid=(n // ROW_TILE,),
        in_specs=[
            pl.BlockSpec((ROW_TILE, d), row),
            pl.BlockSpec((d, d), fixed),
            pl.BlockSpec((ROW_TILE, d), row),
            pl.BlockSpec((None, N_MOD, d), lambda i: (i // tiles_per_seq, 0, 0)),
            pl.BlockSpec((1, d), fixed),
            pl.BlockSpec((d, LANES), fixed),
        ],
        out_specs=[
            pl.BlockSpec((ROW_TILE, d), row),
            pl.BlockSpec((ROW_TILE, d), row),
            pl.BlockSpec((ROW_TILE, LANES), row),
        ],
        out_shape=[
            jax.ShapeDtypeStruct((n, d), F32),
            jax.ShapeDtypeStruct((n, d), BF16),
            jax.ShapeDtypeStruct((n, LANES), F32),
        ],
        compiler_params=_params(("arbitrary",)),
        name="mix_out",
    )(m, w, x, mod, gain, w_router)


def _route_kernel(lg_ref, aff_ref, pos_ref, rank_ref, *, cap):
    lg = lg_ref[...]
    rows = lg.shape[1]
    mx = jnp.max(lg, axis=0, keepdims=True)
    ex = jnp.exp(lg - mx)
    aff = ex / jnp.sum(ex, axis=0, keepdims=True)
    aff_ref[...] = aff

    def count(mask):
        c = jnp.sum(mask.astype(jnp.int32), axis=2, keepdims=True)
        return jnp.sum(c, axis=1, keepdims=True)

    def as_float(bits):
        return lax.bitcast_convert_type(bits, F32)

    def search(i, thr):
        cand = thr | jnp.left_shift(jnp.int32(1), 30 - i)
        return jnp.where(count(aff >= as_float(cand)) >= cap, cand, thr)

    thr = as_float(lax.fori_loop(0, 31, search, jnp.zeros((N_EXPERTS, 1, 1), jnp.int32)))
    gt = aff > thr
    eq = aff == thr
    need = cap - count(gt)

    a = lax.broadcasted_iota(jnp.int32, (LANES, LANES), 0)
    b = lax.broadcasted_iota(jnp.int32, (LANES, LANES), 1)
    upper = (a <= b).astype(BF16)
    ones = jnp.ones((LANES, LANES), BF16)
    ra = lax.broadcasted_iota(jnp.int32, (rows, rows), 0)
    rb = lax.broadcasted_iota(jnp.int32, (rows, rows), 1)
    lower = (rb < ra).astype(BF16)

    def excl_cumsum(mask):
        mb = mask.astype(BF16)
        within = _dot(mb, upper)
        before = _dot(lower, _dot(mb, ones).astype(BF16))
        return (within + before).astype(jnp.int32) - mask.astype(jnp.int32)

    for e in range(N_EXPERTS):
        sel = gt[e] | (eq[e] & (excl_cumsum(eq[e]) < need[e]))
        rank = excl_cumsum(sel)
        rank_ref[e] = rank
        pos_ref[e] = jnp.where(sel, rank, -1)


def _route(logits_t, cap):
    shape = logits_t.shape
    spec = pl.BlockSpec(shape, lambda: (0, 0, 0))
    return pl.pallas_call(
        functools.partial(_route_kernel, cap=cap),
        in_specs=[spec],
        out_specs=[spec, spec, spec],
        out_shape=[jax.ShapeDtypeStruct(shape, F32), jax.ShapeDtypeStruct(shape, jnp.int32),
                   jax.ShapeDtypeStruct(shape, jnp.int32)],
        compiler_params=pltpu.CompilerParams(vmem_limit_bytes=VMEM_LIMIT),
        name="route",
    )(logits_t)


TOK_TILE = 256


def _invert_kernel(base_ref, end_ref, pos_ref, aff_ref, idx_ref, gate_ref, *, ntiles):
    i = pl.program_id(0)

    @pl.when(i == 0)
    def _():
        idx_ref[...] = jnp.zeros_like(idx_ref)
        gate_ref[...] = jnp.zeros_like(gate_ref)

    local = lax.broadcasted_iota(jnp.int32, (1, TOK_TILE), 1).astype(F32)
    slot = lax.broadcasted_iota(jnp.int32, (LANES, TOK_TILE), 0)
    tile_start = (i * TOK_TILE).astype(F32)
    for e in range(N_EXPERTS):
        p = pos_ref[e:e + 1, :]
        g = aff_ref[e:e + 1, :]
        g_hi = g.astype(BF16)
        r1 = g - g_hi.astype(F32)
        g_mid = r1.astype(BF16)
        g_lo = (r1 - g_mid.astype(F32)).astype(BF16)
        zero = jnp.zeros((11, TOK_TILE), BF16)
        lhs = jnp.concatenate([local.astype(BF16), jnp.ones((1, TOK_TILE), BF16), g_hi, g_mid, g_lo, zero], axis=0)

        def chunk(c, carry):
            hit = jnp.where(p - c * LANES == slot, 1.0, 0.0).astype(BF16)
            r = _dot_nt(lhs, hit)
            idx_ref[e, pl.ds(c, 1), :] += r[0:1] + tile_start * r[1:2]
            gate_ref[e, pl.ds(c, 1), :] += r[2:3] + r[3:4] + r[4:5]
            return carry

        first = base_ref[e * ntiles + i] // LANES
        last = (end_ref[e * ntiles + i] + LANES - 1) // LANES
        lax.fori_loop(first, last, chunk, 0)


def _invert(base, end, pos, aff, cap):
    e, n = pos.shape
    ntiles = n // TOK_TILE
    out_spec = pl.BlockSpec((e, cap // LANES, LANES), lambda i, *_: (0, 0, 0))
    tok_spec = pl.BlockSpec((e, TOK_TILE), lambda i, *_: (0, i))
    idx, gates = pl.pallas_call(
        functools.partial(_invert_kernel, ntiles=ntiles),
        grid_spec=pltpu.PrefetchScalarGridSpec(
            num_scalar_prefetch=2, grid=(ntiles,),
            in_specs=[tok_spec, tok_spec], out_specs=[out_spec, out_spec]),
        out_shape=[jax.ShapeDtypeStruct((e, cap // LANES, LANES), F32)] * 2,
        compiler_params=_params(("arbitrary",)),
        name="slot_lists",
    )(base.reshape(-1), end.reshape(-1), pos, aff)
    return idx.astype(jnp.int32).reshape(e, cap), gates.reshape(e, cap)


WIN_ROWS = 64
WIN_ALIGN = 16


def _combine_kernel(base_ref, end_ref, extra_ref, pos_ref, x1_ref, mod_ref, ye_hbm, o_ref,
                    ystack, wt_ref, acc_ref, xbuf, sems, xsem, *, cap, ntiles):
    i = pl.program_id(0)
    cur = i % 2

    def window_lo(tile, e):
        return (base_ref[e * ntiles + tile] // WIN_ALIGN) * WIN_ALIGN

    def window_copy(tile, e, buf):
        st = pl.multiple_of(jnp.minimum(window_lo(tile, e), cap - WIN_ROWS), WIN_ALIGN)
        return pltpu.make_async_copy(ye_hbm.at[e, pl.ds(st, WIN_ROWS), :],
                                     ystack.at[buf, pl.ds(e * WIN_ROWS, WIN_ROWS), :], sems.at[buf])

    @pl.when(i == 0)
    def _():
        for e in range(N_EXPERTS):
            window_copy(0, e, 0).start()

    @pl.when(i + 1 < ntiles)
    def _():
        for e in range(N_EXPERTS):
            window_copy(i + 1, e, 1 - cur).start()

    slot = lax.broadcasted_iota(jnp.int32, (WIN_ROWS, TOK_TILE), 0)

    def one_hot(p, lo):
        st = jnp.minimum(lo, cap - WIN_ROWS)
        hit = (p - st == slot) & (p >= lo) & (p < lo + WIN_ROWS)
        return jnp.where(hit, 1.0, 0.0).astype(BF16)

    for e in range(N_EXPERTS):
        wt_ref[e * WIN_ROWS:(e + 1) * WIN_ROWS, :] = one_hot(pos_ref[e:e + 1, :], window_lo(i, e))
    for e in range(N_EXPERTS):
        window_copy(i, e, cur).wait()
    acc_ref[...] = lax.dot_general(wt_ref[...], ystack[cur], (((0,), (0,)), ((), ())),
                                   preferred_element_type=F32)

    @pl.when(extra_ref[i] > 0)
    def _():
        def per_expert(e, carry):
            lo = window_lo(i, e)
            nwin = (end_ref[e * ntiles + i] - lo + WIN_ROWS - 1) // WIN_ROWS
            p = pos_ref[pl.ds(e, 1), :]

            def per_window(w, c):
                lo_w = lo + w * WIN_ROWS
                st = pl.multiple_of(jnp.minimum(lo_w, cap - WIN_ROWS), WIN_ALIGN)
                cp = pltpu.make_async_copy(ye_hbm.at[e, pl.ds(st, WIN_ROWS), :], xbuf, xsem)
                cp.start()
                cp.wait()
                acc_ref[...] += lax.dot_general(one_hot(p, lo_w), xbuf[...], (((0,), (0,)), ((), ())),
                                                preferred_element_type=F32)
                return c

            lax.fori_loop(1, nwin, per_window, 0)
            return carry

        lax.fori_loop(0, N_EXPERTS, per_expert, 0)

    o_ref[...] = x1_ref[...] + mod_ref[5:6, :] * acc_ref[...]


def _combine(base, end, pos, x1, mod, ye):
    e, n = pos.shape
    cap, d = ye.shape[1], ye.shape[2]
    ntiles = n // TOK_TILE
    lo = (base // WIN_ALIGN) * WIN_ALIGN
    extra = jnp.max(jnp.maximum((end - lo + WIN_ROWS - 1) // WIN_ROWS - 1, 0), axis=0).astype(jnp.int32)
    tiles_per_seq = SEQ // TOK_TILE
    return pl.pallas_call(
        functools.partial(_combine_kernel, cap=cap, ntiles=ntiles),
        grid_spec=pltpu.PrefetchScalarGridSpec(
            num_scalar_prefetch=3, grid=(ntiles,),
            in_specs=[
                pl.BlockSpec((e, TOK_TILE), lambda i, *_: (0, i)),
                pl.BlockSpec((TOK_TILE, d), lambda i, *_: (i, 0)),
                pl.BlockSpec((None, N_MOD, d), lambda i, *_: (i // tiles_per_seq, 0, 0)),
                pl.BlockSpec(memory_space=pl.ANY),
            ],
            out_specs=pl.BlockSpec((TOK_TILE, d), lambda i, *_: (i, 0)),
            scratch_shapes=[
                pltpu.VMEM((2, e * WIN_ROWS, d), BF16),
                pltpu.VMEM((e * WIN_ROWS, TOK_TILE), BF16),
                pltpu.VMEM((TOK_TILE, d), F32),
                pltpu.VMEM((WIN_ROWS, d), BF16),
                pltpu.SemaphoreType.DMA((2,)),
                pltpu.SemaphoreType.DMA(()),
            ]),
        out_shape=jax.ShapeDtypeStruct((n, d), F32),
        compiler_params=_params(("arbitrary",)),
        name="combine",
    )(base.reshape(-1), end.reshape(-1), extra, pos, x1, mod, ye)


FFN_ROWS = 2048
FFN_SUB = 512
FFN_FT = 512


def _ffn_kernel(x_ref, wg_ref, wu_ref, wd_ref, gate_ref, o_ref, acc_ref):
    f = pl.program_id(2)

    @pl.when(f == 0)
    def _():
        acc_ref[...] = jnp.zeros_like(acc_ref)

    sub = min(FFN_SUB, x_ref.shape[0])
    for t in range(x_ref.shape[0] // sub):
        rows = slice(t * sub, (t + 1) * sub)
        xt = x_ref[rows, :]
        a = _dot(xt, wg_ref[...])
        b = _dot(xt, wu_ref[...])
        acc_ref[rows, :] += _dot((_silu(a) * b).astype(BF16), wd_ref[...])

    @pl.when(f == pl.num_programs(2) - 1)
    def _():
        o_ref[...] = (acc_ref[...] * gate_ref[...]).astype(o_ref.dtype)


def _expert_ffn(xe, w_gate, w_up, w_down, gates):
    e, cap, d = xe.shape
    ff = w_gate.shape[2]
    rb = min(cap, FFN_ROWS)
    return pl.pallas_call(
        _ffn_kernel,
        grid=(e, cap // rb, ff // FFN_FT),
        in_specs=[
            pl.BlockSpec((None, rb, d), lambda i, r, f: (i, r, 0)),
            pl.BlockSpec((None, d, FFN_FT), lambda i, r, f: (i, 0, f)),
            pl.BlockSpec((None, d, FFN_FT), lambda i, r, f: (i, 0, f)),
            pl.BlockSpec((None, FFN_FT, d), lambda i, r, f: (i, f, 0)),
            pl.BlockSpec((None, rb, 1), lambda i, r, f: (i, r, 0)),
        ],
        out_specs=pl.BlockSpec((None, rb, d), lambda i, r, f: (i, r, 0)),
        out_shape=jax.ShapeDtypeStruct((e, cap, d), BF16),
        scratch_shapes=[pltpu.VMEM((rb, d), F32)],
        compiler_params=_params(("arbitrary", "arbitrary", "arbitrary")),
        name="expert_ffn",
    )(xe, w_gate, w_up, w_down, gates)


def _moe(h2, logits, x1, mod, wg, wu, wd):
    n, d = h2.shape
    cap = max(1, EC_FACTOR * n // N_EXPERTS)
    lt = logits[:, :N_EXPERTS].T.reshape(N_EXPERTS, n // LANES, LANES)
    aff, pos, rank = _route(lt, cap)
    aff = aff.reshape(N_EXPERTS, n)
    pos = pos.reshape(N_EXPERTS, n)
    base = rank.reshape(N_EXPERTS, n)[:, ::TOK_TILE]
    end = jnp.concatenate([base[:, 1:], jnp.full((N_EXPERTS, 1), cap, jnp.int32)], axis=1)
    idx, gates = _invert(base, end, pos, aff, cap)
    xe = jnp.take(h2, idx.reshape(-1), axis=0).reshape(N_EXPERTS, cap, d)
    ye = _expert_ffn(xe, wg, wu, wd, gates[..., None])
    return _combine(base, end, pos, x1, mod, ye)


def _trunk(x, c, p):
    b = x.shape[0]
    n = b * SEQ
    mod = _ada(c, p["ada_w"], p["ada_b"])
    xf = x.reshape(n, D_MODEL)
    for i in range(2):
        g1 = p["norm1_g"][i].reshape(1, D_MODEL)
        g2 = p["norm2_g"][i].reshape(1, D_MODEL)
        if i == 0:
            qkv = _norm_qkv(xf, mod[i], g1, p["w_qkv"])
            m = _attention(qkv.reshape(b, SEQ, 3 * D_MODEL), p["q_gain"], p["k_gain"], p["bias"])
            w_mix = p["w_o"]
        else:
            v = _fnet_in(xf, mod[i], g1, p["w_in"], p["cs"])
            m = _seq_dft(v.reshape(b, SEQ, 2 * D_MODEL), p["ct"], p["st_neg"])
            w_mix = p["w_out"]
        x1, h2, logits = _mix_out(m.reshape(n, D_MODEL), w_mix, xf, mod[i], g2, p["w_router"][i])
        xf = _moe(h2, logits, x1, mod[i], p["w_gate"][i], p["w_up"][i], p["w_down"][i])
    return xf.reshape(b, SEQ, D_MODEL)


def kernel(x_prompt, x_sample, c_prompt, c_sample, norm1_g, norm2_g, ada_w, ada_b, na_w_qkv, na_q_g, na_k_g, na_rpb, na_w_o, fn_w_in, fn_w_out, moe_w_router, moe_w_gate, moe_w_up, moe_w_down):
    cs, ct, st_neg = _dft_tables()
    tile2 = lambda g: jnp.tile(g.reshape(1, HEAD_DIM), (1, HEADS_PER_BLOCK))
    p = {
        "norm1_g": norm1_g, "norm2_g": norm2_g, "ada_w": ada_w, "ada_b": ada_b,
        "w_qkv": na_w_qkv[0].astype(BF16),
        "q_gain": tile2(na_q_g[0]) * (HEAD_DIM ** -0.5),
        "k_gain": tile2(na_k_g[0]),
        "bias": _attn_bias_tables(na_rpb[0]),
        "w_o": na_w_o[0].astype(BF16),
        "w_in": fn_w_in[0].astype(BF16),
        "w_out": fn_w_out[0].astype(BF16),
        "cs": cs, "ct": ct, "st_neg": st_neg,
        "w_router": jnp.pad(moe_w_router, ((0, 0), (0, 0), (0, LANES - N_EXPERTS))),
        "w_gate": moe_w_gate.astype(BF16),
        "w_up": moe_w_up.astype(BF16),
        "w_down": moe_w_down.astype(BF16),
    }
    return (_trunk(x_prompt, c_prompt, p), _trunk(x_sample, c_sample, p))
```

```python
import functools

import numpy as np
import jax
import jax.numpy as jnp
from jax import lax
from jax.experimental import pallas as pl
from jax.experimental.pallas import tpu as pltpu

D_MODEL = 1024
SEQ = 2048
GRID_W = 64
ROWS = SEQ // GRID_W
NA_HEADS = 16
HEAD_DIM = D_MODEL // NA_HEADS
WIN_R = 8
WIN_C = 16
FN_GROUPS = 4
FN_GROUP_DIM = D_MODEL // FN_GROUPS
N_EXPERTS = 16
EC_FACTOR = 2
EXPERT_FF = 2048
N_MOD = 6
EPS = 1e-6
NEG = -1e30

LANES = 128
KEY_ROWS = min(WIN_R, ROWS)
KEYS = KEY_ROWS * GRID_W
HEADS_PER_BLOCK = LANES // HEAD_DIM
N_ROW_OFFSETS = KEY_ROWS

VMEM_LIMIT = 52 * 1024 * 1024

F32 = jnp.float32
BF16 = jnp.bfloat16


def _params(sem):
    return pltpu.CompilerParams(dimension_semantics=sem, vmem_limit_bytes=VMEM_LIMIT)


def _dot(a, b):
    return jnp.dot(a, b, preferred_element_type=F32)


def _dot_nt(a, b):
    return lax.dot_general(a, b, (((1,), (1,)), ((), ())), preferred_element_type=F32)


def _dot_split(a, b):
    a_hi = a.astype(BF16)
    a_lo = (a - a_hi.astype(F32)).astype(BF16)
    b_hi = b.astype(BF16)
    b_lo = (b - b_hi.astype(F32)).astype(BF16)
    return _dot(a_hi, b_hi) + (_dot(a_hi, b_lo) + _dot(a_lo, b_hi))


def _silu(a):
    return a * (1.0 / (1.0 + jnp.exp(-a)))


def _rms_mod(x, gain, shift, scale):
    ms = jnp.mean(x * x, axis=-1, keepdims=True)
    y = x * lax.rsqrt(ms + EPS) * gain
    return y * (1.0 + scale) + shift


def _ada_kernel(c_ref, w_ref, b_ref, o_ref):
    o_ref[...] = _dot_split(_silu(c_ref[...]), w_ref[...]) + b_ref[...]


def _ada(c, ada_w, ada_b):
    depth, d, n6 = ada_w.shape
    b = c.shape[0]
    tn = 1536
    out = pl.pallas_call(
        _ada_kernel,
        grid=(depth, n6 // tn),
        in_specs=[
            pl.BlockSpec((b, d), lambda i, j: (0, 0)),
            pl.BlockSpec((None, d, tn), lambda i, j: (i, 0, j)),
            pl.BlockSpec((None, 1, tn), lambda i, j: (i, 0, j)),
        ],
        out_specs=pl.BlockSpec((None, b, tn), lambda i, j: (i, 0, j)),
        out_shape=jax.ShapeDtypeStruct((depth, b, n6), F32),
        compiler_params=_params(("arbitrary", "arbitrary")),
        name="ada_mod",
    )(c, ada_w, ada_b.reshape(depth, 1, n6))
    return out.reshape(depth, b, N_MOD, d)


ROW_TILE = 512


def _qkv_kernel(x_ref, mod_ref, g_ref, w_ref, o_ref):
    h = _rms_mod(x_ref[...], g_ref[...], mod_ref[0:1, :], mod_ref[1:2, :])
    o_ref[...] = _dot(h.astype(BF16), w_ref[...]).astype(BF16)


def _norm_qkv(x, mod, gain, w):
    n, d = x.shape
    nn = w.shape[1]
    tiles_per_seq = SEQ // ROW_TILE
    return pl.pallas_call(
        _qkv_kernel,
        grid=(n // ROW_TILE,),
        in_specs=[
            pl.BlockSpec((ROW_TILE, d), lambda i: (i, 0)),
            pl.BlockSpec((None, N_MOD, d), lambda i: (i // tiles_per_seq, 0, 0)),
            pl.BlockSpec((1, d), lambda i: (0, 0)),
            pl.BlockSpec((d, nn), lambda i: (0, 0)),
        ],
        out_specs=pl.BlockSpec((ROW_TILE, nn), lambda i: (i, 0)),
        out_shape=jax.ShapeDtypeStruct((n, nn), BF16),
        compiler_params=_params(("arbitrary",)),
        name="norm_qkv",
    )(x, mod, gain, w)


def _fnet_in_kernel(x_ref, mod_ref, g_ref, w_ref, cs_ref, o_ref):
    h = _rms_mod(x_ref[...], g_ref[...], mod_ref[0:1, :], mod_ref[1:2, :])
    u = _dot(h.astype(BF16), w_ref[...]).astype(BF16)
    gd = FN_GROUP_DIM
    for g in range(FN_GROUPS):
        v = _dot(u[:, g * gd:(g + 1) * gd], cs_ref[...])
        o_ref[:, g * gd:(g + 1) * gd] = v[:, :gd].astype(BF16)
        o_ref[:, D_MODEL + g * gd:D_MODEL + (g + 1) * gd] = v[:, gd:].astype(BF16)


def _fnet_in(x, mod, gain, w, cs):
    n, d = x.shape
    tiles_per_seq = SEQ // ROW_TILE
    return pl.pallas_call(
        _fnet_in_kernel,
        grid=(n // ROW_TILE,),
        in_specs=[
            pl.BlockSpec((ROW_TILE, d), lambda i: (i, 0)),
            pl.BlockSpec((None, N_MOD, d), lambda i: (i // tiles_per_seq, 0, 0)),
            pl.BlockSpec((1, d), lambda i: (0, 0)),
            pl.BlockSpec((d, d), lambda i: (0, 0)),
            pl.BlockSpec(cs.shape, lambda i: (0, 0)),
        ],
        out_specs=pl.BlockSpec((ROW_TILE, 2 * d), lambda i: (i, 0)),
        out_shape=jax.ShapeDtypeStruct((n, 2 * d), BF16),
        compiler_params=_params(("arbitrary",)),
        name="fnet_in",
    )(x, mod, gain, w, cs)


SEQ_DFT_TILE = 1024
SEQ_DFT_SUB = 256


def _seq_dft_kernel(ct_ref, st_ref, v_ref, o_ref):
    scale = 1.0 / np.sqrt(SEQ)
    for s in range(SEQ_DFT_TILE // SEQ_DFT_SUB):
        rows = slice(s * SEQ_DFT_SUB, (s + 1) * SEQ_DFT_SUB)
        acc = _dot(ct_ref[rows, :], v_ref[:, :D_MODEL]) + _dot(st_ref[rows, :], v_ref[:, D_MODEL:])
        o_ref[rows, :] = (acc * scale).astype(BF16)


def _seq_dft(v, ct, st_neg):
    b = v.shape[0]
    return pl.pallas_call(
        _seq_dft_kernel,
        grid=(SEQ // SEQ_DFT_TILE, b),
        in_specs=[
            pl.BlockSpec((SEQ_DFT_TILE, SEQ), lambda k, i: (k, 0)),
            pl.BlockSpec((SEQ_DFT_TILE, SEQ), lambda k, i: (k, 0)),
            pl.BlockSpec((None, SEQ, 2 * D_MODEL), lambda k, i: (i, 0, 0)),
        ],
        out_specs=pl.BlockSpec((None, SEQ_DFT_TILE, D_MODEL), lambda k, i: (i, k, 0)),
        out_shape=jax.ShapeDtypeStruct((b, SEQ, D_MODEL), BF16),
        compiler_params=_params(("arbitrary", "arbitrary")),
        name="seq_dft",
    )(ct, st_neg, v)


def _dft_tables():
    def cos_sin(n):
        k = lax.broadcasted_iota(jnp.int32, (n, n), 0)
        t = lax.broadcasted_iota(jnp.int32, (n, n), 1)
        ang = ((k * t) % n).astype(F32) * (2.0 * np.pi / n)
        return jnp.cos(ang), jnp.sin(ang)
    cc, sc = cos_sin(FN_GROUP_DIM)
    cs = (jnp.concatenate([cc, sc], axis=1) * (1.0 / np.sqrt(FN_GROUP_DIM))).astype(BF16)
    ct, st = cos_sin(SEQ)
    return cs, ct.astype(BF16), (-st).astype(BF16)


def _attn_bias_tables(rpb):
    q = np.arange(GRID_W)
    kc = np.arange(GRID_W)
    start = np.clip(q - WIN_C // 2, 0, GRID_W - WIN_C)
    rel = kc[None, :] - start[:, None]
    mask = (rel >= 0) & (rel < WIN_C)
    dcol = np.clip(kc[None, :] - q[:, None] + WIN_C - 1, 0, 2 * WIN_C - 2)
    o = np.arange(N_ROW_OFFSETS)
    i = np.arange(KEY_ROWS)
    drow = i[None, :] - o[:, None] + WIN_R - 1
    tbl = rpb[:, drow[:, None, :, None], dcol[None, :, None, :]]
    tbl = jnp.where(jnp.asarray(mask)[None, None, :, None, :], tbl.astype(F32), NEG)
    tbl = tbl.reshape(NA_HEADS // HEADS_PER_BLOCK, HEADS_PER_BLOCK, N_ROW_OFFSETS, GRID_W, KEYS)
    return tbl


ATTN_GROUP = 8


def _attn_kernel(q_ref, k_ref, v_ref, qg_ref, kg_ref, bias_ref, o_ref, qn_ref, kn_ref, s_ref, p_ref):
    lane = lax.broadcasted_iota(jnp.int32, (1, LANES), 1)
    lo = lane < HEAD_DIM

    ia = lax.broadcasted_iota(jnp.int32, (LANES, LANES), 0) // HEAD_DIM
    ib = lax.broadcasted_iota(jnp.int32, (LANES, LANES), 1) // HEAD_DIM
    same_head = jnp.where(ia == ib, 1.0, 0.0).astype(BF16)

    def head_norm(x, g):
        ms = _dot((x * x).astype(BF16), same_head) * (1.0 / HEAD_DIM)
        return x * lax.rsqrt(ms + EPS) * g

    qn_ref[...] = head_norm(q_ref[...].astype(F32), qg_ref[...]).astype(BF16)
    kn_ref[...] = head_norm(k_ref[...].astype(F32), kg_ref[...]).astype(BF16)

    def group(g, carry):
        starts = []
        for j in range(ATTN_GROUP):
            r = g * ATTN_GROUP + j
            rs = jnp.clip(r - KEY_ROWS // 2, 0, ROWS - KEY_ROWS)
            off = r - rs
            k0 = pl.multiple_of(rs * GRID_W, GRID_W)
            q0 = pl.multiple_of(r * GRID_W, GRID_W)
            starts.append((k0, q0))
            kr = kn_ref[pl.ds(k0, KEYS), :]
            qr = qn_ref[pl.ds(q0, GRID_W), :]
            for h in range(HEADS_PER_BLOCK):
                keep = lo if h == 0 else jnp.logical_not(lo)
                qm = jnp.where(keep, qr, jnp.zeros_like(qr))
                s_ref[j * HEADS_PER_BLOCK + h] = _dot_nt(qm, kr) + bias_ref[h, off]
        inv = []
        for u in range(ATTN_GROUP * HEADS_PER_BLOCK):
            s = s_ref[u]
            p = jnp.exp(s - jnp.max(s, axis=-1, keepdims=True))
            inv.append(1.0 / jnp.sum(p, axis=-1, keepdims=True))
            p_ref[u] = p.astype(BF16)
        for j in range(ATTN_GROUP):
            k0, q0 = starts[j]
            vr = v_ref[pl.ds(k0, KEYS), :]
            outs = [_dot(p_ref[j * HEADS_PER_BLOCK + h], vr) * inv[j * HEADS_PER_BLOCK + h]
                    for h in range(HEADS_PER_BLOCK)]
            o_ref[pl.ds(q0, GRID_W), :] = jnp.where(lo, outs[0], outs[1]).astype(BF16)
        return carry

    lax.fori_loop(0, ROWS // ATTN_GROUP, group, 0)


def _attention(qkv, q_gain, k_gain, bias):
    b = qkv.shape[0]
    nblk = D_MODEL // LANES
    blk = lambda base: pl.BlockSpec((None, SEQ, LANES), lambda hp, i: (i, 0, base + hp))
    return pl.pallas_call(
        _attn_kernel,
        grid=(nblk, b),
        in_specs=[
            blk(0), blk(nblk), blk(2 * nblk),
            pl.BlockSpec((1, LANES), lambda hp, i: (0, 0)),
            pl.BlockSpec((1, LANES), lambda hp, i: (0, 0)),
            pl.BlockSpec((None, HEADS_PER_BLOCK, N_ROW_OFFSETS, GRID_W, KEYS),
                         lambda hp, i: (hp, 0, 0, 0, 0)),
        ],
        out_specs=pl.BlockSpec((None, SEQ, LANES), lambda hp, i: (i, 0, hp)),
        out_shape=jax.ShapeDtypeStruct((b, SEQ, D_MODEL), BF16),
        scratch_shapes=[
            pltpu.VMEM((SEQ, LANES), BF16), pltpu.VMEM((SEQ, LANES), BF16),
            pltpu.VMEM((ATTN_GROUP * HEADS_PER_BLOCK, GRID_W, KEYS), F32),
            pltpu.VMEM((ATTN_GROUP * HEADS_PER_BLOCK, GRID_W, KEYS), BF16),
        ],
        compiler_params=_params(("arbitrary", "arbitrary")),
        name="nbr_attention",
    )(qkv, qkv, qkv, q_gain, k_gain, bias)


def _mix_out_kernel(m_ref, w_ref, x_ref, mod_ref, g_ref, wr_ref, x1_ref, h2_ref, lg_ref):
    x1 = x_ref[...] + mod_ref[2:3, :] * _dot(m_ref[...], w_ref[...])
    x1_ref[...] = x1
    h2 = _rms_mod(x1, g_ref[...], mod_ref[3:4, :], mod_ref[4:5, :])
    h2_ref[...] = h2
    lg_ref[...] = _dot_split(h2, wr_ref[...])


def _mix_out(m, w, x, mod, gain, w_router):
    n, d = x.shape
    tiles_per_seq = SEQ // ROW_TILE
    row = lambda i: (i, 0)
    fixed = lambda i: (0, 0)
    return pl.pallas_call(
        _mix_out_kernel,
        grid=(n // ROW_TILE,),
        in_specs=[
            pl.BlockSpec((ROW_TILE, d), row),
            pl.BlockSpec((d, d), fixed),
            pl.BlockSpec((ROW_TILE, d), row),
            pl.BlockSpec((None, N_MOD, d), lambda i: (i // tiles_per_seq, 0, 0)),
            pl.BlockSpec((1, d), fixed),
            pl.BlockSpec((d, LANES), fixed),
        ],
        out_specs=[
            pl.BlockSpec((ROW_TILE, d), row),
            pl.BlockSpec((ROW_TILE, d), row),
            pl.BlockSpec((ROW_TILE, LANES), row),
        ],
        out_shape=[
            jax.ShapeDtypeStruct((n, d), F32),
            jax.ShapeDtypeStruct((n, d), F32),
            jax.ShapeDtypeStruct((n, LANES), F32),
        ],
        compiler_params=_params(("arbitrary",)),
        name="mix_out",
    )(m, w, x, mod, gain, w_router)


def _route_kernel(lg_ref, aff_ref, pos_ref, rank_ref, *, cap):
    lg = lg_ref[...]
    rows = lg.shape[1]
    mx = jnp.max(lg, axis=0, keepdims=True)
    ex = jnp.exp(lg - mx)
    aff = ex / jnp.sum(ex, axis=0, keepdims=True)
    aff_ref[...] = aff

    def count(mask):
        c = jnp.sum(mask.astype(jnp.int32), axis=2, keepdims=True)
        return jnp.sum(c, axis=1, keepdims=True)

    def as_float(bits):
        return lax.bitcast_convert_type(bits, F32)

    def search(i, thr):
        cand = thr | jnp.left_shift(jnp.int32(1), 30 - i)
        return jnp.where(count(aff >= as_float(cand)) >= cap, cand, thr)

    thr = as_float(lax.fori_loop(0, 31, search, jnp.zeros((N_EXPERTS, 1, 1), jnp.int32)))
    gt = aff > thr
    eq = aff == thr
    need = cap - count(gt)

    a = lax.broadcasted_iota(jnp.int32, (LANES, LANES), 0)
    b = lax.broadcasted_iota(jnp.int32, (LANES, LANES), 1)
    upper = (a <= b).astype(BF16)
    ones = jnp.ones((LANES, LANES), BF16)
    ra = lax.broadcasted_iota(jnp.int32, (rows, rows), 0)
    rb = lax.broadcasted_iota(jnp.int32, (rows, rows), 1)
    lower = (rb < ra).astype(BF16)

    def excl_cumsum(mask):
        mb = mask.astype(BF16)
        within = _dot(mb, upper)
        before = _dot(lower, _dot(mb, ones).astype(BF16))
        return (within + before).astype(jnp.int32) - mask.astype(jnp.int32)

    for e in range(N_EXPERTS):
        sel = gt[e] | (eq[e] & (excl_cumsum(eq[e]) < need[e]))
        rank = excl_cumsum(sel)
        rank_ref[e] = rank
        pos_ref[e] = jnp.where(sel, rank, -1)


def _route(logits_t, cap):
    shape = logits_t.shape
    spec = pl.BlockSpec(shape, lambda: (0, 0, 0))
    return pl.pallas_call(
        functools.partial(_route_kernel, cap=cap),
        in_specs=[spec],
        out_specs=[spec, spec, spec],
        out_shape=[jax.ShapeDtypeStruct(shape, F32), jax.ShapeDtypeStruct(shape, jnp.int32),
                   jax.ShapeDtypeStruct(shape, jnp.int32)],
        compiler_params=pltpu.CompilerParams(vmem_limit_bytes=VMEM_LIMIT),
        name="route",
    )(logits_t)


TOK_TILE = 256


def _invert_kernel(base_ref, end_ref, extra_ref, pos_ref, aff_ref, idx_ref, gate_ref, *, ntiles):
    i = pl.program_id(0)
    nchunks = idx_ref.shape[1]

    @pl.when(i == 0)
    def _():
        idx_ref[...] = jnp.zeros_like(idx_ref)
        gate_ref[...] = jnp.zeros_like(gate_ref)

    local = lax.broadcasted_iota(jnp.int32, (1, TOK_TILE), 1).astype(F32)
    slot = lax.broadcasted_iota(jnp.int32, (LANES, TOK_TILE), 0)
    tile_start = (i * TOK_TILE).astype(F32)
    zero = jnp.zeros((11, TOK_TILE), BF16)

    def lhs_rows(g):
        g_hi = g.astype(BF16)
        r1 = g - g_hi.astype(F32)
        g_mid = r1.astype(BF16)
        g_lo = (r1 - g_mid.astype(F32)).astype(BF16)
        return jnp.concatenate([local.astype(BF16), jnp.ones((1, TOK_TILE), BF16), g_hi, g_mid, g_lo, zero], axis=0)

    def add_chunk(e, c, p, lhs):
        hit = jnp.where(p - c * LANES == slot, 1.0, 0.0).astype(BF16)
        r = _dot_nt(lhs, hit)
        row = pl.ds(jnp.minimum(c, nchunks - 1), 1)
        idx_ref[e, row, :] += r[0:1] + tile_start * r[1:2]
        gate_ref[e, row, :] += r[2:3] + r[3:4] + r[4:5]

    for e in range(N_EXPERTS):
        p = pos_ref[e:e + 1, :]
        lhs = lhs_rows(aff_ref[e:e + 1, :])
        first = base_ref[e * ntiles + i] // LANES
        add_chunk(e, first, p, lhs)
        add_chunk(e, first + 1, p, lhs)

    @pl.when(extra_ref[i] > 0)
    def _():
        def per_expert(e, carry):
            p = pos_ref[pl.ds(e, 1), :]
            lhs = lhs_rows(aff_ref[pl.ds(e, 1), :])
            first = base_ref[e * ntiles + i] // LANES
            last = (end_ref[e * ntiles + i] + LANES - 1) // LANES

            def per_chunk(c, c2):
                add_chunk(e, c, p, lhs)
                return c2

            lax.fori_loop(first + 2, last, per_chunk, 0)
            return carry

        lax.fori_loop(0, N_EXPERTS, per_expert, 0)


def _invert(base, end, pos, aff, cap):
    e, n = pos.shape
    ntiles = n // TOK_TILE
    nchunks_tile = (end + LANES - 1) // LANES - base // LANES
    extra = jnp.max(jnp.maximum(nchunks_tile - 2, 0), axis=0).astype(jnp.int32)
    out_spec = pl.BlockSpec((e, cap // LANES, LANES), lambda i, *_: (0, 0, 0))
    tok_spec = pl.BlockSpec((e, TOK_TILE), lambda i, *_: (0, i))
    idx, gates = pl.pallas_call(
        functools.partial(_invert_kernel, ntiles=ntiles),
        grid_spec=pltpu.PrefetchScalarGridSpec(
            num_scalar_prefetch=3, grid=(ntiles,),
            in_specs=[tok_spec, tok_spec], out_specs=[out_spec, out_spec]),
        out_shape=[jax.ShapeDtypeStruct((e, cap // LANES, LANES), F32)] * 2,
        compiler_params=_params(("arbitrary",)),
        name="slot_lists",
    )(base.reshape(-1), end.reshape(-1), extra, pos, aff)
    return idx.astype(jnp.int32).reshape(e * cap), gates.reshape(e, cap)


WIN_ROWS = 64
WIN_ALIGN = 16


def _combine_kernel(base_ref, end_ref, extra_ref, pos_ref, x1_ref, mod_ref, ye_hbm, o_ref,
                    ystack, wt_ref, acc_ref, xbuf, sems, xsem, *, cap, ntiles):
    i = pl.program_id(0)
    cur = i % 2

    def window_lo(tile, e):
        return (base_ref[e * ntiles + tile] // WIN_ALIGN) * WIN_ALIGN

    def window_copy(tile, e, buf):
        st = pl.multiple_of(jnp.minimum(window_lo(tile, e), cap - WIN_ROWS), WIN_ALIGN)
        return pltpu.make_async_copy(ye_hbm.at[e, pl.ds(st, WIN_ROWS), :],
                                     ystack.at[buf, pl.ds(e * WIN_ROWS, WIN_ROWS), :], sems.at[buf])

    @pl.when(i == 0)
    def _():
        for e in range(N_EXPERTS):
            window_copy(0, e, 0).start()

    @pl.when(i + 1 < ntiles)
    def _():
        for e in range(N_EXPERTS):
            window_copy(i + 1, e, 1 - cur).start()

    slot = lax.broadcasted_iota(jnp.int32, (WIN_ROWS, TOK_TILE), 0)

    def one_hot(p, lo):
        st = jnp.minimum(lo, cap - WIN_ROWS)
        hit = (p - st == slot) & (p >= lo) & (p < lo + WIN_ROWS)
        return jnp.where(hit, 1.0, 0.0).astype(BF16)

    for e in range(N_EXPERTS):
        wt_ref[e * WIN_ROWS:(e + 1) * WIN_ROWS, :] = one_hot(pos_ref[e:e + 1, :], window_lo(i, e))
    for e in range(N_EXPERTS):
        window_copy(i, e, cur).wait()
    acc_ref[...] = lax.dot_general(wt_ref[...], ystack[cur], (((0,), (0,)), ((), ())),
                                   preferred_element_type=F32)

    @pl.when(extra_ref[i] > 0)
    def _():
        def per_expert(e, carry):
            lo = window_lo(i, e)
            nwin = (end_ref[e * ntiles + i] - lo + WIN_ROWS - 1) // WIN_ROWS
            p = pos_ref[pl.ds(e, 1), :]

            def per_window(w, c):
                lo_w = lo + w * WIN_ROWS
                st = pl.multiple_of(jnp.minimum(lo_w, cap - WIN_ROWS), WIN_ALIGN)
                cp = pltpu.make_async_copy(ye_hbm.at[e, pl.ds(st, WIN_ROWS), :], xbuf, xsem)
                cp.start()
                cp.wait()
                acc_ref[...] += lax.dot_general(one_hot(p, lo_w), xbuf[...], (((0,), (0,)), ((), ())),
                                                preferred_element_type=F32)
                return c

            lax.fori_loop(1, nwin, per_window, 0)
            return carry

        lax.fori_loop(0, N_EXPERTS, per_expert, 0)

    o_ref[...] = x1_ref[...] + mod_ref[5:6, :] * acc_ref[...]


def _combine(base, end, pos, x1, mod, ye):
    e, n = pos.shape
    cap, d = ye.shape[1], ye.shape[2]
    ntiles = n // TOK_TILE
    lo = (base // WIN_ALIGN) * WIN_ALIGN
    extra = jnp.max(jnp.maximum((end - lo + WIN_ROWS - 1) // WIN_ROWS - 1, 0), axis=0).astype(jnp.int32)
    tiles_per_seq = SEQ // TOK_TILE
    return pl.pallas_call(
        functools.partial(_combine_kernel, cap=cap, ntiles=ntiles),
        grid_spec=pltpu.PrefetchScalarGridSpec(
            num_scalar_prefetch=3, grid=(ntiles,),
            in_specs=[
                pl.BlockSpec((e, TOK_TILE), lambda i, *_: (0, i)),
                pl.BlockSpec((TOK_TILE, d), lambda i, *_: (i, 0)),
                pl.BlockSpec((None, N_MOD, d), lambda i, *_: (i // tiles_per_seq, 0, 0)),
                pl.BlockSpec(memory_space=pl.ANY),
            ],
            out_specs=pl.BlockSpec((TOK_TILE, d), lambda i, *_: (i, 0)),
            scratch_shapes=[
                pltpu.VMEM((2, e * WIN_ROWS, d), BF16),
                pltpu.VMEM((e * WIN_ROWS, TOK_TILE), BF16),
                pltpu.VMEM((TOK_TILE, d), F32),
                pltpu.VMEM((WIN_ROWS, d), BF16),
                pltpu.SemaphoreType.DMA((2,)),
                pltpu.SemaphoreType.DMA(()),
            ]),
        out_shape=jax.ShapeDtypeStruct((n, d), F32),
        compiler_params=_params(("arbitrary",)),
        name="combine",
    )(base.reshape(-1), end.reshape(-1), extra, pos, x1, mod, ye)


FFN_ROWS = 2048
FFN_SUB = 512
FFN_FT = 512


def _ffn_kernel(idx_ref, idx_next_ref, h_hbm, wg_ref, wu_ref, wd_ref, gate_ref, o_ref,
                xbuf, xb_ref, acc_ref, sems, *, nblk, nf):
    f = pl.program_id(2)
    blk = pl.program_id(0) * pl.num_programs(1) + pl.program_id(1)
    cur = blk % 2
    rows_blk = xb_ref.shape[0]
    per_step = rows_blk // nf

    def row_copy(idx, j, buf):
        return pltpu.make_async_copy(h_hbm.at[pl.ds(idx[j], 1), :], xbuf.at[buf, pl.ds(j, 1), :], sems.at[buf])

    def wait_block(buf):
        pltpu.make_async_copy(h_hbm.at[pl.ds(0, rows_blk), :], xbuf.at[buf], sems.at[buf]).wait()

    @pl.when((blk == 0) & (f == 0))
    def _():
        def issue(j, carry):
            row_copy(idx_ref, j, 0).start()
            return carry
        lax.fori_loop(0, rows_blk, issue, 0)

    @pl.when(f == 0)
    def _():
        wait_block(cur)
        xb_ref[...] = xbuf[cur].astype(BF16)
        acc_ref[...] = jnp.zeros_like(acc_ref)

    j0 = f * per_step
    for j in range(per_step):
        row_copy(idx_next_ref, j0 + j, 1 - cur).start()

    sub = min(FFN_SUB, rows_blk)
    for t in range(rows_blk // sub):
        rows = slice(t * sub, (t + 1) * sub)
        xt = xb_ref[rows, :]
        a = _dot(xt, wg_ref[...])
        b = _dot(xt, wu_ref[...])
        acc_ref[rows, :] += _dot((_silu(a) * b).astype(BF16), wd_ref[...])

    @pl.when(f == nf - 1)
    def _():
        o_ref[...] = (acc_ref[...] * gate_ref[...]).astype(o_ref.dtype)

    @pl.when((blk == nblk - 1) & (f == nf - 1))
    def _():
        wait_block(1 - cur)


def _expert_ffn(idx, h, w_gate, w_up, w_down, gates):
    e, cap = gates.shape[0], gates.shape[1]
    d = h.shape[1]
    ff = w_gate.shape[2]
    rb = min(cap, FFN_ROWS)
    nrb = cap // rb
    nblk = e * nrb
    nf = ff // FFN_FT
    return pl.pallas_call(
        functools.partial(_ffn_kernel, nblk=nblk, nf=nf),
        grid=(e, nrb, nf),
        in_specs=[
            pl.BlockSpec((rb,), lambda i, r, f: (i * nrb + r,), memory_space=pltpu.SMEM),
            pl.BlockSpec((rb,), lambda i, r, f: (jnp.minimum(i * nrb + r + 1, nblk - 1),),
                         memory_space=pltpu.SMEM),
            pl.BlockSpec(memory_space=pl.ANY),
            pl.BlockSpec((None, d, FFN_FT), lambda i, r, f: (i, 0, f)),
            pl.BlockSpec((None, d, FFN_FT), lambda i, r, f: (i, 0, f)),
            pl.BlockSpec((None, FFN_FT, d), lambda i, r, f: (i, f, 0)),
            pl.BlockSpec((None, rb, 1), lambda i, r, f: (i, r, 0)),
        ],
        out_specs=pl.BlockSpec((None, rb, d), lambda i, r, f: (i, r, 0)),
        out_shape=jax.ShapeDtypeStruct((e, cap, d), BF16),
        scratch_shapes=[
            pltpu.VMEM((2, rb, d), F32),
            pltpu.VMEM((rb, d), BF16),
            pltpu.VMEM((rb, d), F32),
            pltpu.SemaphoreType.DMA((2,)),
        ],
        compiler_params=_params(("arbitrary", "arbitrary", "arbitrary")),
        name="expert_ffn",
    )(idx, idx, h, w_gate, w_up, w_down, gates)


def _moe(h2, logits, x1, mod, wg, wu, wd):
    n, d = h2.shape
    cap = max(1, EC_FACTOR * n // N_EXPERTS)
    lt = logits[:, :N_EXPERTS].T.reshape(N_EXPERTS, n // LANES, LANES)
    aff, pos, rank = _route(lt, cap)
    aff = aff.reshape(N_EXPERTS, n)
    pos = pos.reshape(N_EXPERTS, n)
    base = rank.reshape(N_EXPERTS, n)[:, ::TOK_TILE]
    end = jnp.concatenate([base[:, 1:], jnp.full((N_EXPERTS, 1), cap, jnp.int32)], axis=1)
    idx, gates = _invert(base, end, pos, aff, cap)
    ye = _expert_ffn(idx, h2, wg, wu, wd, gates[..., None])
    return _combine(base, end, pos, x1, mod, ye)


def _trunk(x, c, p):
    b = x.shape[0]
    n = b * SEQ
    mod = _ada(c, p["ada_w"], p["ada_b"])
    xf = x.reshape(n, D_MODEL)
    for i in range(2):
        g1 = p["norm1_g"][i].reshape(1, D_MODEL)
        g2 = p["norm2_g"][i].reshape(1, D_MODEL)
        if i == 0:
            qkv = _norm_qkv(xf, mod[i], g1, p["w_qkv"])
            m = _attention(qkv.reshape(b, SEQ, 3 * D_MODEL), p["q_gain"], p["k_gain"], p["bias"])
            w_mix = p["w_o"]
        else:
            v = _fnet_in(xf, mod[i], g1, p["w_in"], p["cs"])
            m = _seq_dft(v.reshape(b, SEQ, 2 * D_MODEL), p["ct"], p["st_neg"])
            w_mix = p["w_out"]
        x1, h2, logits = _mix_out(m.reshape(n, D_MODEL), w_mix, xf, mod[i], g2, p["w_router"][i])
        xf = _moe(h2, logits, x1, mod[i], p["w_gate"][i], p["w_up"][i], p["w_down"][i])
    return xf.reshape(b, SEQ, D_MODEL)


def kernel(x_prompt, x_sample, c_prompt, c_sample, norm1_g, norm2_g, ada_w, ada_b, na_w_qkv, na_q_g, na_k_g, na_rpb, na_w_o, fn_w_in, fn_w_out, moe_w_router, moe_w_gate, moe_w_up, moe_w_down):
    cs, ct, st_neg = _dft_tables()
    tile2 = lambda g: jnp.tile(g.reshape(1, HEAD_DIM), (1, HEADS_PER_BLOCK))
    p = {
        "norm1_g": norm1_g, "norm2_g": norm2_g, "ada_w": ada_w, "ada_b": ada_b,
        "w_qkv": na_w_qkv[0].astype(BF16),
        "q_gain": tile2(na_q_g[0]) * (HEAD_DIM ** -0.5),
        "k_gain": tile2(na_k_g[0]),
        "bias": _attn_bias_tables(na_rpb[0]),
        "w_o": na_w_o[0].astype(BF16),
        "w_in": fn_w_in[0].astype(BF16),
        "w_out": fn_w_out[0].astype(BF16),
        "cs": cs, "ct": ct, "st_neg": st_neg,
        "w_router": jnp.pad(moe_w_router, ((0, 0), (0, 0), (0, LANES - N_EXPERTS))),
        "w_gate": moe_w_gate.astype(BF16),
        "w_up": moe_w_up.astype(BF16),
        "w_down": moe_w_down.astype(BF16),
    }
    return (_trunk(x_prompt, c_prompt, p), _trunk(x_sample, c_sample, p))
```

```python
import functools

import numpy as np
import jax
import jax.numpy as jnp
from jax import lax
from jax.experimental import pallas as pl
from jax.experimental.pallas import tpu as pltpu

D_MODEL = 1024
SEQ = 2048
GRID_W = 64
ROWS = SEQ // GRID_W
NA_HEADS = 16
HEAD_DIM = D_MODEL // NA_HEADS
WIN_R = 8
WIN_C = 16
FN_GROUPS = 4
FN_GROUP_DIM = D_MODEL // FN_GROUPS
N_EXPERTS = 16
EC_FACTOR = 2
EXPERT_FF = 2048
N_MOD = 6
EPS = 1e-6
NEG = -1e30

LANES = 128
KEY_ROWS = min(WIN_R, ROWS)
KEYS = KEY_ROWS * GRID_W
HEADS_PER_BLOCK = LANES // HEAD_DIM
N_ROW_OFFSETS = KEY_ROWS

VMEM_LIMIT = 52 * 1024 * 1024

F32 = jnp.float32
BF16 = jnp.bfloat16


def _params(sem):
    return pltpu.CompilerParams(dimension_semantics=sem, vmem_limit_bytes=VMEM_LIMIT)


def _dot(a, b):
    return jnp.dot(a, b, preferred_element_type=F32)


def _dot_nt(a, b):
    return lax.dot_general(a, b, (((1,), (1,)), ((), ())), preferred_element_type=F32)


def _dot_split(a, b):
    a_hi = a.astype(BF16)
    a_lo = (a - a_hi.astype(F32)).astype(BF16)
    b_hi = b.astype(BF16)
    b_lo = (b - b_hi.astype(F32)).astype(BF16)
    return _dot(a_hi, b_hi) + (_dot(a_hi, b_lo) + _dot(a_lo, b_hi))


def _silu(a):
    return a * (1.0 / (1.0 + jnp.exp(-a)))


def _rms_mod(x, gain, shift, scale):
    ms = jnp.mean(x * x, axis=-1, keepdims=True)
    y = x * lax.rsqrt(ms + EPS) * gain
    return y * (1.0 + scale) + shift


def _ada_kernel(c_ref, w_ref, b_ref, o_ref):
    o_ref[...] = _dot_split(_silu(c_ref[...]), w_ref[...]) + b_ref[...]


def _ada(c, ada_w, ada_b):
    depth, d, n6 = ada_w.shape
    b = c.shape[0]
    tn = 1536
    out = pl.pallas_call(
        _ada_kernel,
        grid=(depth, n6 // tn),
        in_specs=[
            pl.BlockSpec((b, d), lambda i, j: (0, 0)),
            pl.BlockSpec((None, d, tn), lambda i, j: (i, 0, j)),
            pl.BlockSpec((None, 1, tn), lambda i, j: (i, 0, j)),
        ],
        out_specs=pl.BlockSpec((None, b, tn), lambda i, j: (i, 0, j)),
        out_shape=jax.ShapeDtypeStruct((depth, b, n6), F32),
        compiler_params=_params(("arbitrary", "arbitrary")),
        name="ada_mod",
    )(c, ada_w, ada_b.reshape(depth, 1, n6))
    return out.reshape(depth, b, N_MOD, d)


ROW_TILE = 512


def _qkv_kernel(x_ref, mod_ref, g_ref, w_ref, o_ref):
    h = _rms_mod(x_ref[...], g_ref[...], mod_ref[0:1, :], mod_ref[1:2, :])
    o_ref[...] = _dot(h.astype(BF16), w_ref[...]).astype(BF16)


def _norm_qkv(x, mod, gain, w):
    n, d = x.shape
    nn = w.shape[1]
    tiles_per_seq = SEQ // ROW_TILE
    return pl.pallas_call(
        _qkv_kernel,
        grid=(n // ROW_TILE,),
        in_specs=[
            pl.BlockSpec((ROW_TILE, d), lambda i: (i, 0)),
            pl.BlockSpec((None, N_MOD, d), lambda i: (i // tiles_per_seq, 0, 0)),
            pl.BlockSpec((1, d), lambda i: (0, 0)),
            pl.BlockSpec((d, nn), lambda i: (0, 0)),
        ],
        out_specs=pl.BlockSpec((ROW_TILE, nn), lambda i: (i, 0)),
        out_shape=jax.ShapeDtypeStruct((n, nn), BF16),
        compiler_params=_params(("arbitrary",)),
        name="norm_qkv",
    )(x, mod, gain, w)


def _fnet_in_kernel(x_ref, mod_ref, g_ref, w_ref, cs_ref, o_ref):
    h = _rms_mod(x_ref[...], g_ref[...], mod_ref[0:1, :], mod_ref[1:2, :])
    u = _dot(h.astype(BF16), w_ref[...]).astype(BF16)
    gd = FN_GROUP_DIM
    for g in range(FN_GROUPS):
        v = _dot(u[:, g * gd:(g + 1) * gd], cs_ref[...])
        o_ref[:, g * gd:(g + 1) * gd] = v[:, :gd].astype(BF16)
        o_ref[:, D_MODEL + g * gd:D_MODEL + (g + 1) * gd] = v[:, gd:].astype(BF16)


def _fnet_in(x, mod, gain, w, cs):
    n, d = x.shape
    tiles_per_seq = SEQ // ROW_TILE
    return pl.pallas_call(
        _fnet_in_kernel,
        grid=(n // ROW_TILE,),
        in_specs=[
            pl.BlockSpec((ROW_TILE, d), lambda i: (i, 0)),
            pl.BlockSpec((None, N_MOD, d), lambda i: (i // tiles_per_seq, 0, 0)),
            pl.BlockSpec((1, d), lambda i: (0, 0)),
            pl.BlockSpec((d, d), lambda i: (0, 0)),
            pl.BlockSpec(cs.shape, lambda i: (0, 0)),
        ],
        out_specs=pl.BlockSpec((ROW_TILE, 2 * d), lambda i: (i, 0)),
        out_shape=jax.ShapeDtypeStruct((n, 2 * d), BF16),
        compiler_params=_params(("arbitrary",)),
        name="fnet_in",
    )(x, mod, gain, w, cs)


SEQ_DFT_TILE = 1024
SEQ_DFT_SUB = 256


def _seq_dft_kernel(ct_ref, st_ref, v_ref, o_ref):
    scale = 1.0 / np.sqrt(SEQ)
    for s in range(SEQ_DFT_TILE // SEQ_DFT_SUB):
        rows = slice(s * SEQ_DFT_SUB, (s + 1) * SEQ_DFT_SUB)
        acc = _dot(ct_ref[rows, :], v_ref[:, :D_MODEL]) + _dot(st_ref[rows, :], v_ref[:, D_MODEL:])
        o_ref[rows, :] = (acc * scale).astype(BF16)


def _seq_dft(v, ct, st_neg):
    b = v.shape[0]
    return pl.pallas_call(
        _seq_dft_kernel,
        grid=(SEQ // SEQ_DFT_TILE, b),
        in_specs=[
            pl.BlockSpec((SEQ_DFT_TILE, SEQ), lambda k, i: (k, 0)),
            pl.BlockSpec((SEQ_DFT_TILE, SEQ), lambda k, i: (k, 0)),
            pl.BlockSpec((None, SEQ, 2 * D_MODEL), lambda k, i: (i, 0, 0)),
        ],
        out_specs=pl.BlockSpec((None, SEQ_DFT_TILE, D_MODEL), lambda k, i: (i, k, 0)),
        out_shape=jax.ShapeDtypeStruct((b, SEQ, D_MODEL), BF16),
        compiler_params=_params(("arbitrary", "arbitrary")),
        name="seq_dft",
    )(ct, st_neg, v)


def _dft_tables():
    def cos_sin(n):
        k = lax.broadcasted_iota(jnp.int32, (n, n), 0)
        t = lax.broadcasted_iota(jnp.int32, (n, n), 1)
        ang = ((k * t) % n).astype(F32) * (2.0 * np.pi / n)
        return jnp.cos(ang), jnp.sin(ang)
    cc, sc = cos_sin(FN_GROUP_DIM)
    cs = (jnp.concatenate([cc, sc], axis=1) * (1.0 / np.sqrt(FN_GROUP_DIM))).astype(BF16)
    ct, st = cos_sin(SEQ)
    return cs, ct.astype(BF16), (-st).astype(BF16)


def _bias_cols_kernel(rpb_ref, select_ref, valid_ref, o_ref):
    r = rpb_ref[...]
    hi = r.astype(BF16)
    r1 = r - hi.astype(F32)
    mid = r1.astype(BF16)
    low = (r1 - mid.astype(F32)).astype(BF16)
    sel = select_ref[...]
    t = _dot(hi, sel) + (_dot(mid, sel) + _dot(low, sel))
    o_ref[...] = jnp.where(valid_ref[...] > 0.0, t, NEG)


def _attn_bias_tables(rpb):
    h, nr, nc = rpb.shape
    q = np.arange(GRID_W)
    kc = np.arange(GRID_W)
    start = np.clip(q - WIN_C // 2, 0, GRID_W - WIN_C)
    rel = kc[None, :] - start[:, None]
    mask = (rel >= 0) & (rel < WIN_C)
    dcol = np.clip(kc[None, :] - q[:, None] + WIN_C - 1, 0, 2 * WIN_C - 2)
    pairs = GRID_W * GRID_W
    select = np.zeros((LANES, pairs), np.float32)
    select[dcol.reshape(-1), np.arange(pairs)] = 1.0
    cols = pl.pallas_call(
        _bias_cols_kernel,
        out_shape=jax.ShapeDtypeStruct((h * nr, pairs), F32),
        name="bias_cols",
    )(jnp.pad(rpb.reshape(h * nr, nc).astype(F32), ((0, 0), (0, LANES - nc))),
      jnp.asarray(select, BF16), jnp.asarray(mask.reshape(1, pairs), F32))
    cols = cols.reshape(h, nr, GRID_W, GRID_W)
    tbl = jnp.stack([cols[:, WIN_R - 1 - o:WIN_R - 1 - o + KEY_ROWS] for o in range(N_ROW_OFFSETS)], axis=1)
    tbl = tbl.transpose(0, 1, 3, 2, 4)
    return tbl.reshape(NA_HEADS // HEADS_PER_BLOCK, HEADS_PER_BLOCK, N_ROW_OFFSETS, GRID_W, KEYS)


ATTN_GROUP = 8


def _attn_kernel(q_ref, k_ref, v_ref, qg_ref, kg_ref, bias_ref, o_ref, qn_ref, kn_ref, s_ref, p_ref):
    lane = lax.broadcasted_iota(jnp.int32, (1, LANES), 1)
    lo = lane < HEAD_DIM

    ia = lax.broadcasted_iota(jnp.int32, (LANES, LANES), 0) // HEAD_DIM
    ib = lax.broadcasted_iota(jnp.int32, (LANES, LANES), 1) // HEAD_DIM
    same_head = jnp.where(ia == ib, 1.0, 0.0).astype(BF16)

    def head_norm(x, g):
        ms = _dot((x * x).astype(BF16), same_head) * (1.0 / HEAD_DIM)
        return x * lax.rsqrt(ms + EPS) * g

    qn_ref[...] = head_norm(q_ref[...].astype(F32), qg_ref[...]).astype(BF16)
    kn_ref[...] = head_norm(k_ref[...].astype(F32), kg_ref[...]).astype(BF16)

    def group(g, carry):
        starts = []
        for j in range(ATTN_GROUP):
            r = g * ATTN_GROUP + j
            rs = jnp.clip(r - KEY_ROWS // 2, 0, ROWS - KEY_ROWS)
            off = r - rs
            k0 = pl.multiple_of(rs * GRID_W, GRID_W)
            q0 = pl.multiple_of(r * GRID_W, GRID_W)
            starts.append((k0, q0))
            kr = kn_ref[pl.ds(k0, KEYS), :]
            qr = qn_ref[pl.ds(q0, GRID_W), :]
            for h in range(HEADS_PER_BLOCK):
                keep = lo if h == 0 else jnp.logical_not(lo)
                qm = jnp.where(keep, qr, jnp.zeros_like(qr))
                s_ref[j * HEADS_PER_BLOCK + h] = _dot_nt(qm, kr) + bias_ref[h, off]
        inv = []
        for u in range(ATTN_GROUP * HEADS_PER_BLOCK):
            s = s_ref[u]
            p = jnp.exp(s - jnp.max(s, axis=-1, keepdims=True))
            inv.append(1.0 / jnp.sum(p, axis=-1, keepdims=True))
            p_ref[u] = p.astype(BF16)
        for j in range(ATTN_GROUP):
            k0, q0 = starts[j]
            vr = v_ref[pl.ds(k0, KEYS), :]
            outs = [_dot(p_ref[j * HEADS_PER_BLOCK + h], vr) * inv[j * HEADS_PER_BLOCK + h]
                    for h in range(HEADS_PER_BLOCK)]
            o_ref[pl.ds(q0, GRID_W), :] = jnp.where(lo, outs[0], outs[1]).astype(BF16)
        return carry

    lax.fori_loop(0, ROWS // ATTN_GROUP, group, 0)


def _attention(qkv, q_gain, k_gain, bias):
    b = qkv.shape[0]
    nblk = D_MODEL // LANES
    blk = lambda base: pl.BlockSpec((None, SEQ, LANES), lambda hp, i: (i, 0, base + hp))
    return pl.pallas_call(
        _attn_kernel,
        grid=(nblk, b),
        in_specs=[
            blk(0), blk(nblk), blk(2 * nblk),
            pl.BlockSpec((1, LANES), lambda hp, i: (0, 0)),
            pl.BlockSpec((1, LANES), lambda hp, i: (0, 0)),
            pl.BlockSpec((None, HEADS_PER_BLOCK, N_ROW_OFFSETS, GRID_W, KEYS),
                         lambda hp, i: (hp, 0, 0, 0, 0)),
        ],
        out_specs=pl.BlockSpec((None, SEQ, LANES), lambda hp, i: (i, 0, hp)),
        out_shape=jax.ShapeDtypeStruct((b, SEQ, D_MODEL), BF16),
        scratch_shapes=[
            pltpu.VMEM((SEQ, LANES), BF16), pltpu.VMEM((SEQ, LANES), BF16),
            pltpu.VMEM((ATTN_GROUP * HEADS_PER_BLOCK, GRID_W, KEYS), F32),
            pltpu.VMEM((ATTN_GROUP * HEADS_PER_BLOCK, GRID_W, KEYS), BF16),
        ],
        compiler_params=_params(("arbitrary", "arbitrary")),
        name="nbr_attention",
    )(qkv, qkv, qkv, q_gain, k_gain, bias)


def _mix_out_kernel(m_ref, w_ref, x_ref, mod_ref, g_ref, wr_ref, x1_ref, h2_ref, lg_ref):
    x1 = x_ref[...] + mod_ref[2:3, :] * _dot(m_ref[...], w_ref[...])
    x1_ref[...] = x1
    h2 = _rms_mod(x1, g_ref[...], mod_ref[3:4, :], mod_ref[4:5, :])
    h2_ref[...] = h2
    lg_ref[...] = _dot_split(h2, wr_ref[...])


def _mix_out(m, w, x, mod, gain, w_router):
    n, d = x.shape
    tiles_per_seq = SEQ // ROW_TILE
    row = lambda i: (i, 0)
    fixed = lambda i: (0, 0)
    return pl.pallas_call(
        _mix_out_kernel,
        grid=(n // ROW_TILE,),
        in_specs=[
            pl.BlockSpec((ROW_TILE, d), row),
            pl.BlockSpec((d, d), fixed),
            pl.BlockSpec((ROW_TILE, d), row),
            pl.BlockSpec((None, N_MOD, d), lambda i: (i // tiles_per_seq, 0, 0)),
            pl.BlockSpec((1, d), fixed),
            pl.BlockSpec((d, LANES), fixed),
        ],
        out_specs=[
            pl.BlockSpec((ROW_TILE, d), row),
            pl.BlockSpec((ROW_TILE, d), row),
            pl.BlockSpec((ROW_TILE, LANES), row),
        ],
        out_shape=[
            jax.ShapeDtypeStruct((n, d), F32),
            jax.ShapeDtypeStruct((n, d), F32),
            jax.ShapeDtypeStruct((n, LANES), F32),
        ],
        compiler_params=_params(("arbitrary",)),
        name="mix_out",
    )(m, w, x, mod, gain, w_router)


def _route_kernel(lg_ref, aff_ref, pos_ref, rank_ref, *, cap):
    lg = lg_ref[...]
    rows = lg.shape[1]
    mx = jnp.max(lg, axis=0, keepdims=True)
    ex = jnp.exp(lg - mx)
    aff = ex / jnp.sum(ex, axis=0, keepdims=True)
    aff_ref[...] = aff

    def count(mask):
        c = jnp.sum(mask.astype(jnp.int32), axis=2, keepdims=True)
        return jnp.sum(c, axis=1, keepdims=True)

    def as_float(bits):
        return lax.bitcast_convert_type(bits, F32)

    def search(i, thr):
        cand = thr | jnp.left_shift(jnp.int32(1), 30 - i)
        return jnp.where(count(aff >= as_float(cand)) >= cap, cand, thr)

    thr = as_float(lax.fori_loop(0, 31, search, jnp.zeros((N_EXPERTS, 1, 1), jnp.int32)))
    gt = aff > thr
    eq = aff == thr
    need = cap - count(gt)

    a = lax.broadcasted_iota(jnp.int32, (LANES, LANES), 0)
    b = lax.broadcasted_iota(jnp.int32, (LANES, LANES), 1)
    upper = (a <= b).astype(BF16)
    ones = jnp.ones((LANES, LANES), BF16)
    ra = lax.broadcasted_iota(jnp.int32, (rows, rows), 0)
    rb = lax.broadcasted_iota(jnp.int32, (rows, rows), 1)
    lower = (rb < ra).astype(BF16)

    def excl_cumsum(mask):
        mb = mask.astype(BF16)
        within = _dot(mb, upper)
        before = _dot(lower, _dot(mb, ones).astype(BF16))
        return (within + before).astype(jnp.int32) - mask.astype(jnp.int32)

    for e in range(N_EXPERTS):
        sel = gt[e] | (eq[e] & (excl_cumsum(eq[e]) < need[e]))
        rank = excl_cumsum(sel)
        rank_ref[e] = rank
        pos_ref[e] = jnp.where(sel, rank, -1)


def _route(logits_t, cap):
    shape = logits_t.shape
    spec = pl.BlockSpec(shape, lambda: (0, 0, 0))
    return pl.pallas_call(
        functools.partial(_route_kernel, cap=cap),
        in_specs=[spec],
        out_specs=[spec, spec, spec],
        out_shape=[jax.ShapeDtypeStruct(shape, F32), jax.ShapeDtypeStruct(shape, jnp.int32),
                   jax.ShapeDtypeStruct(shape, jnp.int32)],
        compiler_params=pltpu.CompilerParams(vmem_limit_bytes=VMEM_LIMIT),
        name="route",
    )(logits_t)


TOK_TILE = 256


def _invert_kernel(base_ref, end_ref, extra_ref, pos_ref, aff_ref, idx_ref, gate_ref, *, ntiles):
    i = pl.program_id(0)
    nchunks = idx_ref.shape[1]

    @pl.when(i == 0)
    def _():
        idx_ref[...] = jnp.zeros_like(idx_ref)
        gate_ref[...] = jnp.zeros_like(gate_ref)

    local = lax.broadcasted_iota(jnp.int32, (1, TOK_TILE), 1).astype(F32)
    slot = lax.broadcasted_iota(jnp.int32, (LANES, TOK_TILE), 0)
    tile_start = (i * TOK_TILE).astype(F32)
    zero = jnp.zeros((11, TOK_TILE), BF16)

    def lhs_rows(g):
        g_hi = g.astype(BF16)
        r1 = g - g_hi.astype(F32)
        g_mid = r1.astype(BF16)
        g_lo = (r1 - g_mid.astype(F32)).astype(BF16)
        return jnp.concatenate([local.astype(BF16), jnp.ones((1, TOK_TILE), BF16), g_hi, g_mid, g_lo, zero], axis=0)

    def add_chunk(e, c, p, lhs):
        hit = jnp.where(p - c * LANES == slot, 1.0, 0.0).astype(BF16)
        r = _dot_nt(lhs, hit)
        row = pl.ds(jnp.minimum(c, nchunks - 1), 1)
        idx_ref[e, row, :] += r[0:1] + tile_start * r[1:2]
        gate_ref[e, row, :] += r[2:3] + r[3:4] + r[4:5]

    for e in range(N_EXPERTS):
        p = pos_ref[e:e + 1, :]
        lhs = lhs_rows(aff_ref[e:e + 1, :])
        first = base_ref[e * ntiles + i] // LANES
        add_chunk(e, first, p, lhs)
        add_chunk(e, first + 1, p, lhs)

    @pl.when(extra_ref[i] > 0)
    def _():
        def per_expert(e, carry):
            p = pos_ref[pl.ds(e, 1), :]
            lhs = lhs_rows(aff_ref[pl.ds(e, 1), :])
            first = base_ref[e * ntiles + i] // LANES
            last = (end_ref[e * ntiles + i] + LANES - 1) // LANES

            def per_chunk(c, c2):
                add_chunk(e, c, p, lhs)
                return c2

            lax.fori_loop(first + 2, last, per_chunk, 0)
            return carry

        lax.fori_loop(0, N_EXPERTS, per_expert, 0)


def _invert(base, end, pos, aff, cap):
    e, n = pos.shape
    ntiles = n // TOK_TILE
    nchunks_tile = (end + LANES - 1) // LANES - base // LANES
    extra = jnp.max(jnp.maximum(nchunks_tile - 2, 0), axis=0).astype(jnp.int32)
    out_spec = pl.BlockSpec((e, cap // LANES, LANES), lambda i, *_: (0, 0, 0))
    tok_spec = pl.BlockSpec((e, TOK_TILE), lambda i, *_: (0, i))
    idx, gates = pl.pallas_call(
        functools.partial(_invert_kernel, ntiles=ntiles),
        grid_spec=pltpu.PrefetchScalarGridSpec(
            num_scalar_prefetch=3, grid=(ntiles,),
            in_specs=[tok_spec, tok_spec], out_specs=[out_spec, out_spec]),
        out_shape=[jax.ShapeDtypeStruct((e, cap // LANES, LANES), F32)] * 2,
        compiler_params=_params(("arbitrary",)),
        name="slot_lists",
    )(base.reshape(-1), end.reshape(-1), extra, pos, aff)
    return idx.astype(jnp.int32).reshape(e * cap), gates.reshape(e, cap)


WIN_ROWS = 64
WIN_ALIGN = 16
WIN_BUFFERS = 3


def _combine_kernel(base_ref, end_ref, extra_ref, pos_ref, x1_ref, mod_ref, ye_hbm, o_ref,
                    ystack, wt_ref, acc_ref, xbuf, sems, xsem, *, cap, ntiles):
    i = pl.program_id(0)
    cur = i % WIN_BUFFERS

    def window_lo(tile, e):
        return (base_ref[e * ntiles + tile] // WIN_ALIGN) * WIN_ALIGN

    def window_copy(tile, e, buf):
        st = pl.multiple_of(jnp.minimum(window_lo(tile, e), cap - WIN_ROWS), WIN_ALIGN)
        return pltpu.make_async_copy(ye_hbm.at[e, pl.ds(st, WIN_ROWS), :],
                                     ystack.at[buf, pl.ds(e * WIN_ROWS, WIN_ROWS), :], sems.at[buf])

    @pl.when(i == 0)
    def _():
        for t in range(min(WIN_BUFFERS - 1, ntiles)):
            for e in range(N_EXPERTS):
                window_copy(t, e, t).start()

    @pl.when(i + WIN_BUFFERS - 1 < ntiles)
    def _():
        for e in range(N_EXPERTS):
            window_copy(i + WIN_BUFFERS - 1, e, (i + WIN_BUFFERS - 1) % WIN_BUFFERS).start()

    slot = lax.broadcasted_iota(jnp.int32, (WIN_ROWS, TOK_TILE), 0)

    def one_hot(p, lo):
        st = jnp.minimum(lo, cap - WIN_ROWS)
        hit = (p - st == slot) & (p >= lo) & (p < lo + WIN_ROWS)
        return jnp.where(hit, 1.0, 0.0).astype(BF16)

    for e in range(N_EXPERTS):
        wt_ref[e * WIN_ROWS:(e + 1) * WIN_ROWS, :] = one_hot(pos_ref[e:e + 1, :], window_lo(i, e))
    for e in range(N_EXPERTS):
        window_copy(i, e, cur).wait()
    acc_ref[...] = lax.dot_general(wt_ref[...], ystack[cur], (((0,), (0,)), ((), ())),
                                   preferred_element_type=F32)

    @pl.when(extra_ref[i] > 0)
    def _():
        def per_expert(e, carry):
            lo = window_lo(i, e)
            nwin = (end_ref[e * ntiles + i] - lo + WIN_ROWS - 1) // WIN_ROWS
            p = pos_ref[pl.ds(e, 1), :]

            def per_window(w, c):
                lo_w = lo + w * WIN_ROWS
                st = pl.multiple_of(jnp.minimum(lo_w, cap - WIN_ROWS), WIN_ALIGN)
                cp = pltpu.make_async_copy(ye_hbm.at[e, pl.ds(st, WIN_ROWS), :], xbuf, xsem)
                cp.start()
                cp.wait()
                acc_ref[...] += lax.dot_general(one_hot(p, lo_w), xbuf[...], (((0,), (0,)), ((), ())),
                                                preferred_element_type=F32)
                return c

            lax.fori_loop(1, nwin, per_window, 0)
            return carry

        lax.fori_loop(0, N_EXPERTS, per_expert, 0)

    o_ref[...] = x1_ref[...] + mod_ref[5:6, :] * acc_ref[...]


def _combine(base, end, pos, x1, mod, ye):
    e, n = pos.shape
    cap, d = ye.shape[1], ye.shape[2]
    ntiles = n // TOK_TILE
    lo = (base // WIN_ALIGN) * WIN_ALIGN
    extra = jnp.max(jnp.maximum((end - lo + WIN_ROWS - 1) // WIN_ROWS - 1, 0), axis=0).astype(jnp.int32)
    tiles_per_seq = SEQ // TOK_TILE
    return pl.pallas_call(
        functools.partial(_combine_kernel, cap=cap, ntiles=ntiles),
        grid_spec=pltpu.PrefetchScalarGridSpec(
            num_scalar_prefetch=3, grid=(ntiles,),
            in_specs=[
                pl.BlockSpec((e, TOK_TILE), lambda i, *_: (0, i)),
                pl.BlockSpec((TOK_TILE, d), lambda i, *_: (i, 0)),
                pl.BlockSpec((None, N_MOD, d), lambda i, *_: (i // tiles_per_seq, 0, 0)),
                pl.BlockSpec(memory_space=pl.ANY),
            ],
            out_specs=pl.BlockSpec((TOK_TILE, d), lambda i, *_: (i, 0)),
            scratch_shapes=[
                pltpu.VMEM((WIN_BUFFERS, e * WIN_ROWS, d), BF16),
                pltpu.VMEM((e * WIN_ROWS, TOK_TILE), BF16),
                pltpu.VMEM((TOK_TILE, d), F32),
                pltpu.VMEM((WIN_ROWS, d), BF16),
                pltpu.SemaphoreType.DMA((WIN_BUFFERS,)),
                pltpu.SemaphoreType.DMA(()),
            ]),
        out_shape=jax.ShapeDtypeStruct((n, d), F32),
        compiler_params=_params(("arbitrary",)),
        name="combine",
    )(base.reshape(-1), end.reshape(-1), extra, pos, x1, mod, ye)


FFN_ROWS = 2048
FFN_SUB = 512
FFN_FT = 512


def _ffn_kernel(idx_ref, idx_next_ref, h_hbm, wg_ref, wu_ref, wd_ref, gate_ref, o_ref,
                xbuf, xb_ref, acc_ref, sem, *, nblk, nf):
    f = pl.program_id(2)
    blk = pl.program_id(0) * pl.num_programs(1) + pl.program_id(1)
    rows_blk = xb_ref.shape[0]
    per_step = rows_blk // nf

    def row_copy(idx, k, j):
        return pltpu.make_async_copy(h_hbm.at[pl.ds(idx[k * per_step + j], 1), :],
                                     xbuf.at[k, pl.ds(j, 1), :], sem)

    def wait_block():
        for k in range(nf):
            pltpu.make_async_copy(h_hbm.at[pl.ds(0, per_step), :], xbuf.at[k], sem).wait()

    @pl.when((blk == 0) & (f == 0))
    def _():
        for k in range(nf):
            def issue(j, carry):
                row_copy(idx_ref, k, j).start()
                return carry
            lax.fori_loop(0, per_step, issue, 0)

    @pl.when(f == 0)
    def _():
        wait_block()
        for k in range(nf):
            xb_ref[k * per_step:(k + 1) * per_step, :] = xbuf[k].astype(BF16)
        acc_ref[...] = jnp.zeros_like(acc_ref)

    for j in range(per_step):
        row_copy(idx_next_ref, f, j).start()

    sub = min(FFN_SUB, rows_blk)
    for t in range(rows_blk // sub):
        rows = slice(t * sub, (t + 1) * sub)
        xt = xb_ref[rows, :]
        a = _dot(xt, wg_ref[...])
        b = _dot(xt, wu_ref[...])
        acc_ref[rows, :] += _dot((_silu(a) * b).astype(BF16), wd_ref[...])

    @pl.when(f == nf - 1)
    def _():
        o_ref[...] = (acc_ref[...] * gate_ref[...]).astype(o_ref.dtype)

    @pl.when((blk == nblk - 1) & (f == nf - 1))
    def _():
        wait_block()


def _expert_ffn(idx, h, w_gate, w_up, w_down, gates):
    e, cap = gates.shape[0], gates.shape[1]
    d = h.shape[1]
    ff = w_gate.shape[2]
    rb = min(cap, FFN_ROWS)
    nrb = cap // rb
    nblk = e * nrb
    nf = ff // FFN_FT
    return pl.pallas_call(
        functools.partial(_ffn_kernel, nblk=nblk, nf=nf),
        grid=(e, nrb, nf),
        in_specs=[
            pl.BlockSpec((rb,), lambda i, r, f: (i * nrb + r,), memory_space=pltpu.SMEM),
            pl.BlockSpec((rb,), lambda i, r, f: (jnp.minimum(i * nrb + r + 1, nblk - 1),),
                         memory_space=pltpu.SMEM),
            pl.BlockSpec(memory_space=pl.ANY),
            pl.BlockSpec((None, d, FFN_FT), lambda i, r, f: (i, 0, f)),
            pl.BlockSpec((None, d, FFN_FT), lambda i, r, f: (i, 0, f)),
            pl.BlockSpec((None, FFN_FT, d), lambda i, r, f: (i, f, 0)),
            pl.BlockSpec((None, rb, 1), lambda i, r, f: (i, r, 0)),
        ],
        out_specs=pl.BlockSpec((None, rb, d), lambda i, r, f: (i, r, 0)),
        out_shape=jax.ShapeDtypeStruct((e, cap, d), BF16),
        scratch_shapes=[
            pltpu.VMEM((nf, rb // nf, d), F32),
            pltpu.VMEM((rb, d), BF16),
            pltpu.VMEM((rb, d), F32),
            pltpu.SemaphoreType.DMA(()),
        ],
        compiler_params=_params(("arbitrary", "arbitrary", "arbitrary")),
        name="expert_ffn",
    )(idx, idx, h, w_gate, w_up, w_down, gates)


def _moe(h2, logits, x1, mod, wg, wu, wd):
    n, d = h2.shape
    cap = max(1, EC_FACTOR * n // N_EXPERTS)
    lt = logits[:, :N_EXPERTS].T.reshape(N_EXPERTS, n // LANES, LANES)
    aff, pos, rank = _route(lt, cap)
    aff = aff.reshape(N_EXPERTS, n)
    pos = pos.reshape(N_EXPERTS, n)
    base = rank.reshape(N_EXPERTS, n)[:, ::TOK_TILE]
    end = jnp.concatenate([base[:, 1:], jnp.full((N_EXPERTS, 1), cap, jnp.int32)], axis=1)
    idx, gates = _invert(base, end, pos, aff, cap)
    ye = _expert_ffn(idx, h2, wg, wu, wd, gates[..., None])
    return _combine(base, end, pos, x1, mod, ye)


def _trunk(x, c, p):
    b = x.shape[0]
    n = b * SEQ
    mod = _ada(c, p["ada_w"], p["ada_b"])
    xf = x.reshape(n, D_MODEL)
    for i in range(2):
        g1 = p["norm1_g"][i].reshape(1, D_MODEL)
        g2 = p["norm2_g"][i].reshape(1, D_MODEL)
        if i == 0:
            qkv = _norm_qkv(xf, mod[i], g1, p["w_qkv"])
            m = _attention(qkv.reshape(b, SEQ, 3 * D_MODEL), p["q_gain"], p["k_gain"], p["bias"])
            w_mix = p["w_o"]
        else:
            v = _fnet_in(xf, mod[i], g1, p["w_in"], p["cs"])
            m = _seq_dft(v.reshape(b, SEQ, 2 * D_MODEL), p["ct"], p["st_neg"])
            w_mix = p["w_out"]
        x1, h2, logits = _mix_out(m.reshape(n, D_MODEL), w_mix, xf, mod[i], g2, p["w_router"][i])
        xf = _moe(h2, logits, x1, mod[i], p["w_gate"][i], p["w_up"][i], p["w_down"][i])
    return xf.reshape(b, SEQ, D_MODEL)


def kernel(x_prompt, x_sample, c_prompt, c_sample, norm1_g, norm2_g, ada_w, ada_b, na_w_qkv, na_q_g, na_k_g, na_rpb, na_w_o, fn_w_in, fn_w_out, moe_w_router, moe_w_gate, moe_w_up, moe_w_down):
    cs, ct, st_neg = _dft_tables()
    tile2 = lambda g: jnp.tile(g.reshape(1, HEAD_DIM), (1, HEADS_PER_BLOCK))
    p = {
        "norm1_g": norm1_g, "norm2_g": norm2_g, "ada_w": ada_w, "ada_b": ada_b,
        "w_qkv": na_w_qkv[0].astype(BF16),
        "q_gain": tile2(na_q_g[0]) * (HEAD_DIM ** -0.5),
        "k_gain": tile2(na_k_g[0]),
        "bias": _attn_bias_tables(na_rpb[0]),
        "w_o": na_w_o[0].astype(BF16),
        "w_in": fn_w_in[0].astype(BF16),
        "w_out": fn_w_out[0].astype(BF16),
        "cs": cs, "ct": ct, "st_neg": st_neg,
        "w_router": jnp.pad(moe_w_router, ((0, 0), (0, 0), (0, LANES - N_EXPERTS))),
        "w_gate": moe_w_gate.astype(BF16),
        "w_up": moe_w_up.astype(BF16),
        "w_down": moe_w_down.astype(BF16),
    }
    return (_trunk(x_prompt, c_prompt, p), _trunk(x_sample, c_sample, p))
```

```python
import functools

import numpy as np
import jax
import jax.numpy as jnp
from jax import lax
from jax.experimental import pallas as pl
from jax.experimental.pallas import tpu as pltpu

D_MODEL = 1024
SEQ = 2048
GRID_W = 64
ROWS = SEQ // GRID_W
NA_HEADS = 16
HEAD_DIM = D_MODEL // NA_HEADS
WIN_R = 8
WIN_C = 16
FN_GROUPS = 4
FN_GROUP_DIM = D_MODEL // FN_GROUPS
N_EXPERTS = 16
EC_FACTOR = 2
EXPERT_FF = 2048
N_MOD = 6
EPS = 1e-6
NEG = -1e30

LANES = 128
KEY_ROWS = min(WIN_R, ROWS)
KEYS = KEY_ROWS * GRID_W
HEADS_PER_BLOCK = LANES // HEAD_DIM
N_ROW_OFFSETS = KEY_ROWS

VMEM_LIMIT = 52 * 1024 * 1024

F32 = jnp.float32
BF16 = jnp.bfloat16


def _params(sem):
    return pltpu.CompilerParams(dimension_semantics=sem, vmem_limit_bytes=VMEM_LIMIT)


def _dot(a, b):
    return jnp.dot(a, b, preferred_element_type=F32)


def _dot_nt(a, b):
    return lax.dot_general(a, b, (((1,), (1,)), ((), ())), preferred_element_type=F32)


def _dot_split(a, b):
    a_hi = a.astype(BF16)
    a_lo = (a - a_hi.astype(F32)).astype(BF16)
    b_hi = b.astype(BF16)
    b_lo = (b - b_hi.astype(F32)).astype(BF16)
    return _dot(a_hi, b_hi) + (_dot(a_hi, b_lo) + _dot(a_lo, b_hi))


def _silu(a):
    return a * (1.0 / (1.0 + jnp.exp(-a)))


def _rms_mod(x, gain, shift, scale):
    ms = jnp.mean(x * x, axis=-1, keepdims=True)
    y = x * lax.rsqrt(ms + EPS) * gain
    return y * (1.0 + scale) + shift


def _ada_kernel(c_ref, w_ref, b_ref, o_ref):
    o_ref[...] = _dot_split(_silu(c_ref[...]), w_ref[...]) + b_ref[...]


def _ada(c, ada_w, ada_b):
    depth, d, n6 = ada_w.shape
    b = c.shape[0]
    tn = 1536
    out = pl.pallas_call(
        _ada_kernel,
        grid=(depth, n6 // tn),
        in_specs=[
            pl.BlockSpec((b, d), lambda i, j: (0, 0)),
            pl.BlockSpec((None, d, tn), lambda i, j: (i, 0, j)),
            pl.BlockSpec((None, 1, tn), lambda i, j: (i, 0, j)),
        ],
        out_specs=pl.BlockSpec((None, b, tn), lambda i, j: (i, 0, j)),
        out_shape=jax.ShapeDtypeStruct((depth, b, n6), F32),
        compiler_params=_params(("arbitrary", "arbitrary")),
        name="ada_mod",
    )(c, ada_w, ada_b.reshape(depth, 1, n6))
    return out.reshape(depth, b, N_MOD, d)


ROW_TILE = 512


def _qkv_kernel(x_ref, mod_ref, g_ref, w_ref, o_ref):
    h = _rms_mod(x_ref[...], g_ref[...], mod_ref[0:1, :], mod_ref[1:2, :])
    o_ref[...] = _dot(h.astype(BF16), w_ref[...]).astype(BF16)


def _norm_qkv(x, mod, gain, w):
    n, d = x.shape
    nn = w.shape[1]
    tiles_per_seq = SEQ // ROW_TILE
    return pl.pallas_call(
        _qkv_kernel,
        grid=(n // ROW_TILE,),
        in_specs=[
            pl.BlockSpec((ROW_TILE, d), lambda i: (i, 0)),
            pl.BlockSpec((None, N_MOD, d), lambda i: (i // tiles_per_seq, 0, 0)),
            pl.BlockSpec((1, d), lambda i: (0, 0)),
            pl.BlockSpec((d, nn), lambda i: (0, 0)),
        ],
        out_specs=pl.BlockSpec((ROW_TILE, nn), lambda i: (i, 0)),
        out_shape=jax.ShapeDtypeStruct((n, nn), BF16),
        compiler_params=_params(("arbitrary",)),
        name="norm_qkv",
    )(x, mod, gain, w)


def _fnet_in_kernel(x_ref, mod_ref, g_ref, w_ref, cs_ref, o_ref):
    h = _rms_mod(x_ref[...], g_ref[...], mod_ref[0:1, :], mod_ref[1:2, :])
    u = _dot(h.astype(BF16), w_ref[...]).astype(BF16)
    gd = FN_GROUP_DIM
    for g in range(FN_GROUPS):
        v = _dot(u[:, g * gd:(g + 1) * gd], cs_ref[...])
        o_ref[:, g * gd:(g + 1) * gd] = v[:, :gd].astype(BF16)
        o_ref[:, D_MODEL + g * gd:D_MODEL + (g + 1) * gd] = v[:, gd:].astype(BF16)


def _fnet_in(x, mod, gain, w, cs):
    n, d = x.shape
    tiles_per_seq = SEQ // ROW_TILE
    return pl.pallas_call(
        _fnet_in_kernel,
        grid=(n // ROW_TILE,),
        in_specs=[
            pl.BlockSpec((ROW_TILE, d), lambda i: (i, 0)),
            pl.BlockSpec((None, N_MOD, d), lambda i: (i // tiles_per_seq, 0, 0)),
            pl.BlockSpec((1, d), lambda i: (0, 0)),
            pl.BlockSpec((d, d), lambda i: (0, 0)),
            pl.BlockSpec(cs.shape, lambda i: (0, 0)),
        ],
        out_specs=pl.BlockSpec((ROW_TILE, 2 * d), lambda i: (i, 0)),
        out_shape=jax.ShapeDtypeStruct((n, 2 * d), BF16),
        compiler_params=_params(("arbitrary",)),
        name="fnet_in",
    )(x, mod, gain, w, cs)


SEQ_DFT_TILE = 1024
SEQ_DFT_SUB = 256


def _seq_dft_kernel(ct_ref, st_ref, v_ref, o_ref):
    scale = 1.0 / np.sqrt(SEQ)
    for s in range(SEQ_DFT_TILE // SEQ_DFT_SUB):
        rows = slice(s * SEQ_DFT_SUB, (s + 1) * SEQ_DFT_SUB)
        acc = _dot(ct_ref[rows, :], v_ref[:, :D_MODEL]) + _dot(st_ref[rows, :], v_ref[:, D_MODEL:])
        o_ref[rows, :] = (acc * scale).astype(BF16)


def _seq_dft(v, ct, st_neg):
    b = v.shape[0]
    return pl.pallas_call(
        _seq_dft_kernel,
        grid=(SEQ // SEQ_DFT_TILE, b),
        in_specs=[
            pl.BlockSpec((SEQ_DFT_TILE, SEQ), lambda k, i: (k, 0)),
            pl.BlockSpec((SEQ_DFT_TILE, SEQ), lambda k, i: (k, 0)),
            pl.BlockSpec((None, SEQ, 2 * D_MODEL), lambda k, i: (i, 0, 0)),
        ],
        out_specs=pl.BlockSpec((None, SEQ_DFT_TILE, D_MODEL), lambda k, i: (i, k, 0)),
        out_shape=jax.ShapeDtypeStruct((b, SEQ, D_MODEL), BF16),
        compiler_params=_params(("arbitrary", "arbitrary")),
        name="seq_dft",
    )(ct, st_neg, v)


def _dft_tables():
    def cos_sin(n):
        k = lax.broadcasted_iota(jnp.int32, (n, n), 0)
        t = lax.broadcasted_iota(jnp.int32, (n, n), 1)
        ang = ((k * t) % n).astype(F32) * (2.0 * np.pi / n)
        return jnp.cos(ang), jnp.sin(ang)
    cc, sc = cos_sin(FN_GROUP_DIM)
    cs = (jnp.concatenate([cc, sc], axis=1) * (1.0 / np.sqrt(FN_GROUP_DIM))).astype(BF16)
    ct, st = cos_sin(SEQ)
    return cs, ct.astype(BF16), (-st).astype(BF16)


def _bias_cols_kernel(rpb_ref, select_ref, valid_ref, o_ref):
    r = rpb_ref[...]
    hi = r.astype(BF16)
    r1 = r - hi.astype(F32)
    mid = r1.astype(BF16)
    low = (r1 - mid.astype(F32)).astype(BF16)
    sel = select_ref[...]
    t = _dot(hi, sel) + (_dot(mid, sel) + _dot(low, sel))
    o_ref[...] = jnp.where(valid_ref[...] > 0.0, t, NEG)


def _attn_bias_tables(rpb):
    h, nr, nc = rpb.shape
    q = np.arange(GRID_W)
    kc = np.arange(GRID_W)
    start = np.clip(q - WIN_C // 2, 0, GRID_W - WIN_C)
    rel = kc[None, :] - start[:, None]
    mask = (rel >= 0) & (rel < WIN_C)
    dcol = np.clip(kc[None, :] - q[:, None] + WIN_C - 1, 0, 2 * WIN_C - 2)
    pairs = GRID_W * GRID_W
    select = np.zeros((LANES, pairs), np.float32)
    select[dcol.reshape(-1), np.arange(pairs)] = 1.0
    cols = pl.pallas_call(
        _bias_cols_kernel,
        out_shape=jax.ShapeDtypeStruct((h * nr, pairs), F32),
        name="bias_cols",
    )(jnp.pad(rpb.reshape(h * nr, nc).astype(F32), ((0, 0), (0, LANES - nc))),
      jnp.asarray(select, BF16), jnp.asarray(mask.reshape(1, pairs), F32))
    cols = cols.reshape(h, nr, GRID_W, GRID_W)
    tbl = jnp.stack([cols[:, WIN_R - 1 - o:WIN_R - 1 - o + KEY_ROWS] for o in range(N_ROW_OFFSETS)], axis=1)
    tbl = tbl.transpose(0, 1, 3, 2, 4)
    return tbl.reshape(NA_HEADS // HEADS_PER_BLOCK, HEADS_PER_BLOCK, N_ROW_OFFSETS, GRID_W, KEYS)


ATTN_GROUP = 8


def _attn_kernel(q_ref, k_ref, v_ref, qg_ref, kg_ref, bias_ref, o_ref, qn_ref, kn_ref, s_ref, p_ref):
    lane = lax.broadcasted_iota(jnp.int32, (1, LANES), 1)
    lo = lane < HEAD_DIM

    ia = lax.broadcasted_iota(jnp.int32, (LANES, LANES), 0) // HEAD_DIM
    ib = lax.broadcasted_iota(jnp.int32, (LANES, LANES), 1) // HEAD_DIM
    same_head = jnp.where(ia == ib, 1.0, 0.0).astype(BF16)

    def head_norm(x, g):
        ms = _dot((x * x).astype(BF16), same_head) * (1.0 / HEAD_DIM)
        return x * lax.rsqrt(ms + EPS) * g

    qn_ref[...] = head_norm(q_ref[...].astype(F32), qg_ref[...]).astype(BF16)
    kn_ref[...] = head_norm(k_ref[...].astype(F32), kg_ref[...]).astype(BF16)

    def group(g, carry):
        starts = []
        for j in range(ATTN_GROUP):
            r = g * ATTN_GROUP + j
            rs = jnp.clip(r - KEY_ROWS // 2, 0, ROWS - KEY_ROWS)
            off = r - rs
            k0 = pl.multiple_of(rs * GRID_W, GRID_W)
            q0 = pl.multiple_of(r * GRID_W, GRID_W)
            starts.append((k0, q0))
            kr = kn_ref[pl.ds(k0, KEYS), :]
            qr = qn_ref[pl.ds(q0, GRID_W), :]
            for h in range(HEADS_PER_BLOCK):
                keep = lo if h == 0 else jnp.logical_not(lo)
                qm = jnp.where(keep, qr, jnp.zeros_like(qr))
                s_ref[j * HEADS_PER_BLOCK + h] = _dot_nt(qm, kr) + bias_ref[h, off]
        inv = []
        for u in range(ATTN_GROUP * HEADS_PER_BLOCK):
            s = s_ref[u]
            p = jnp.exp(s - jnp.max(s, axis=-1, keepdims=True))
            inv.append(1.0 / jnp.sum(p, axis=-1, keepdims=True))
            p_ref[u] = p.astype(BF16)
        for j in range(ATTN_GROUP):
            k0, q0 = starts[j]
            vr = v_ref[pl.ds(k0, KEYS), :]
            outs = [_dot(p_ref[j * HEADS_PER_BLOCK + h], vr) * inv[j * HEADS_PER_BLOCK + h]
                    for h in range(HEADS_PER_BLOCK)]
            o_ref[pl.ds(q0, GRID_W), :] = jnp.where(lo, outs[0], outs[1]).astype(BF16)
        return carry

    lax.fori_loop(0, ROWS // ATTN_GROUP, group, 0)


def _attention(qkv, q_gain, k_gain, bias):
    b = qkv.shape[0]
    nblk = D_MODEL // LANES
    blk = lambda base: pl.BlockSpec((None, SEQ, LANES), lambda hp, i: (i, 0, base + hp))
    return pl.pallas_call(
        _attn_kernel,
        grid=(nblk, b),
        in_specs=[
            blk(0), blk(nblk), blk(2 * nblk),
            pl.BlockSpec((1, LANES), lambda hp, i: (0, 0)),
            pl.BlockSpec((1, LANES), lambda hp, i: (0, 0)),
            pl.BlockSpec((None, HEADS_PER_BLOCK, N_ROW_OFFSETS, GRID_W, KEYS),
                         lambda hp, i: (hp, 0, 0, 0, 0)),
        ],
        out_specs=pl.BlockSpec((None, SEQ, LANES), lambda hp, i: (i, 0, hp)),
        out_shape=jax.ShapeDtypeStruct((b, SEQ, D_MODEL), BF16),
        scratch_shapes=[
            pltpu.VMEM((SEQ, LANES), BF16), pltpu.VMEM((SEQ, LANES), BF16),
            pltpu.VMEM((ATTN_GROUP * HEADS_PER_BLOCK, GRID_W, KEYS), F32),
            pltpu.VMEM((ATTN_GROUP * HEADS_PER_BLOCK, GRID_W, KEYS), BF16),
        ],
        compiler_params=_params(("arbitrary", "arbitrary")),
        name="nbr_attention",
    )(qkv, qkv, qkv, q_gain, k_gain, bias)


def _mix_out_kernel(m_ref, w_ref, x_ref, mod_ref, g_ref, wr_ref, x1_ref, h2_ref, lg_ref):
    x1 = x_ref[...] + mod_ref[2:3, :] * _dot(m_ref[...], w_ref[...])
    x1_ref[...] = x1
    h2 = _rms_mod(x1, g_ref[...], mod_ref[3:4, :], mod_ref[4:5, :])
    h2_ref[...] = h2
    h_hi = h2.astype(BF16)
    h_lo = (h2 - h_hi.astype(F32)).astype(BF16)
    a = _dot_nt(wr_ref[...], h_hi)
    b = _dot_nt(wr_ref[0:N_EXPERTS, :], h_lo)
    lg_ref[...] = a[0:N_EXPERTS] + (a[N_EXPERTS:] + b)


def _mix_out(m, w, x, mod, gain, w_router):
    n, d = x.shape
    tiles_per_seq = SEQ // ROW_TILE
    row = lambda i: (i, 0)
    fixed = lambda i: (0, 0)
    return pl.pallas_call(
        _mix_out_kernel,
        grid=(n // ROW_TILE,),
        in_specs=[
            pl.BlockSpec((ROW_TILE, d), row),
            pl.BlockSpec((d, d), fixed),
            pl.BlockSpec((ROW_TILE, d), row),
            pl.BlockSpec((None, N_MOD, d), lambda i: (i // tiles_per_seq, 0, 0)),
            pl.BlockSpec((1, d), fixed),
            pl.BlockSpec((2 * N_EXPERTS, d), fixed),
        ],
        out_specs=[
            pl.BlockSpec((ROW_TILE, d), row),
            pl.BlockSpec((ROW_TILE, d), row),
            pl.BlockSpec((N_EXPERTS, ROW_TILE), lambda i: (0, i)),
        ],
        out_shape=[
            jax.ShapeDtypeStruct((n, d), F32),
            jax.ShapeDtypeStruct((n, d), F32),
            jax.ShapeDtypeStruct((N_EXPERTS, n), F32),
        ],
        compiler_params=_params(("arbitrary",)),
        name="mix_out",
    )(m, w, x, mod, gain, w_router)


def _route_kernel(lg_ref, aff_ref, pos_ref, rank_ref, *, cap):
    lg = lg_ref[...]
    rows = lg.shape[1]
    mx = jnp.max(lg, axis=0, keepdims=True)
    ex = jnp.exp(lg - mx)
    aff = ex / jnp.sum(ex, axis=0, keepdims=True)
    aff_ref[...] = aff

    def count(mask):
        c = jnp.sum(mask.astype(jnp.int32), axis=2, keepdims=True)
        return jnp.sum(c, axis=1, keepdims=True)

    def as_float(bits):
        return lax.bitcast_convert_type(bits, F32)

    def search(i, thr):
        cand = thr | jnp.left_shift(jnp.int32(1), 30 - i)
        return jnp.where(count(aff >= as_float(cand)) >= cap, cand, thr)

    thr = as_float(lax.fori_loop(0, 31, search, jnp.zeros((N_EXPERTS, 1, 1), jnp.int32)))
    gt = aff > thr
    eq = aff == thr
    need = cap - count(gt)

    a = lax.broadcasted_iota(jnp.int32, (LANES, LANES), 0)
    b = lax.broadcasted_iota(jnp.int32, (LANES, LANES), 1)
    upper = (a <= b).astype(BF16)
    ones = jnp.ones((LANES, LANES), BF16)
    ra = lax.broadcasted_iota(jnp.int32, (rows, rows), 0)
    rb = lax.broadcasted_iota(jnp.int32, (rows, rows), 1)
    lower = (rb < ra).astype(BF16)

    def excl_cumsum(mask):
        mb = mask.astype(BF16)
        within = _dot(mb, upper)
        before = _dot(lower, _dot(mb, ones).astype(BF16))
        return (within + before).astype(jnp.int32) - mask.astype(jnp.int32)

    for e in range(N_EXPERTS):
        sel = gt[e] | (eq[e] & (excl_cumsum(eq[e]) < need[e]))
        rank = excl_cumsum(sel)
        rank_ref[e] = rank
        pos_ref[e] = jnp.where(sel, rank, -1)


def _route(logits_t, cap):
    shape = logits_t.shape
    spec = pl.BlockSpec(shape, lambda: (0, 0, 0))
    return pl.pallas_call(
        functools.partial(_route_kernel, cap=cap),
        in_specs=[spec],
        out_specs=[spec, spec, spec],
        out_shape=[jax.ShapeDtypeStruct(shape, F32), jax.ShapeDtypeStruct(shape, jnp.int32),
                   jax.ShapeDtypeStruct(shape, jnp.int32)],
        compiler_params=pltpu.CompilerParams(vmem_limit_bytes=VMEM_LIMIT),
        name="route",
    )(logits_t)


TOK_TILE = 256


def _invert_kernel(base_ref, end_ref, extra_ref, pos_ref, aff_ref, idx_ref, gate_ref, *, ntiles):
    i = pl.program_id(0)
    nchunks = idx_ref.shape[1]

    @pl.when(i == 0)
    def _():
        idx_ref[...] = jnp.zeros_like(idx_ref)
        gate_ref[...] = jnp.zeros_like(gate_ref)

    local = lax.broadcasted_iota(jnp.int32, (1, TOK_TILE), 1).astype(F32)
    slot = lax.broadcasted_iota(jnp.int32, (LANES, TOK_TILE), 0)
    tile_start = (i * TOK_TILE).astype(F32)
    zero = jnp.zeros((11, TOK_TILE), BF16)

    def lhs_rows(g):
        g_hi = g.astype(BF16)
        r1 = g - g_hi.astype(F32)
        g_mid = r1.astype(BF16)
        g_lo = (r1 - g_mid.astype(F32)).astype(BF16)
        return jnp.concatenate([local.astype(BF16), jnp.ones((1, TOK_TILE), BF16), g_hi, g_mid, g_lo, zero], axis=0)

    def add_chunk(e, c, p, lhs):
        hit = jnp.where(p - c * LANES == slot, 1.0, 0.0).astype(BF16)
        r = _dot_nt(lhs, hit)
        row = pl.ds(jnp.minimum(c, nchunks - 1), 1)
        idx_ref[e, row, :] += r[0:1] + tile_start * r[1:2]
        gate_ref[e, row, :] += r[2:3] + r[3:4] + r[4:5]

    for e in range(N_EXPERTS):
        p = pos_ref[e:e + 1, :]
        lhs = lhs_rows(aff_ref[e:e + 1, :])
        first = base_ref[e * ntiles + i] // LANES
        add_chunk(e, first, p, lhs)
        add_chunk(e, first + 1, p, lhs)

    @pl.when(extra_ref[i] > 0)
    def _():
        def per_expert(e, carry):
            p = pos_ref[pl.ds(e, 1), :]
            lhs = lhs_rows(aff_ref[pl.ds(e, 1), :])
            first = base_ref[e * ntiles + i] // LANES
            last = (end_ref[e * ntiles + i] + LANES - 1) // LANES

            def per_chunk(c, c2):
                add_chunk(e, c, p, lhs)
                return c2

            lax.fori_loop(first + 2, last, per_chunk, 0)
            return carry

        lax.fori_loop(0, N_EXPERTS, per_expert, 0)


def _invert(base, end, pos, aff, cap):
    e, n = pos.shape
    ntiles = n // TOK_TILE
    nchunks_tile = (end + LANES - 1) // LANES - base // LANES
    extra = jnp.max(jnp.maximum(nchunks_tile - 2, 0), axis=0).astype(jnp.int32)
    out_spec = pl.BlockSpec((e, cap // LANES, LANES), lambda i, *_: (0, 0, 0))
    tok_spec = pl.BlockSpec((e, TOK_TILE), lambda i, *_: (0, i))
    idx, gates = pl.pallas_call(
        functools.partial(_invert_kernel, ntiles=ntiles),
        grid_spec=pltpu.PrefetchScalarGridSpec(
            num_scalar_prefetch=3, grid=(ntiles,),
            in_specs=[tok_spec, tok_spec], out_specs=[out_spec, out_spec]),
        out_shape=[jax.ShapeDtypeStruct((e, cap // LANES, LANES), F32)] * 2,
        compiler_params=_params(("arbitrary",)),
        name="slot_lists",
    )(base.reshape(-1), end.reshape(-1), extra, pos, aff)
    return idx.astype(jnp.int32).reshape(e * cap), gates.reshape(e, cap)


WIN_ROWS = 64
WIN_ALIGN = 16
WIN_BUFFERS = 3


def _combine_kernel(base_ref, end_ref, extra_ref, pos_ref, x1_ref, mod_ref, ye_hbm, o_ref,
                    ystack, wt_ref, acc_ref, xbuf, sems, xsem, *, cap, ntiles):
    i = pl.program_id(0)
    cur = i % WIN_BUFFERS

    def window_lo(tile, e):
        return (base_ref[e * ntiles + tile] // WIN_ALIGN) * WIN_ALIGN

    def window_copy(tile, e, buf):
        st = pl.multiple_of(jnp.minimum(window_lo(tile, e), cap - WIN_ROWS), WIN_ALIGN)
        return pltpu.make_async_copy(ye_hbm.at[e, pl.ds(st, WIN_ROWS), :],
                                     ystack.at[buf, pl.ds(e * WIN_ROWS, WIN_ROWS), :], sems.at[buf])

    @pl.when(i == 0)
    def _():
        for t in range(min(WIN_BUFFERS - 1, ntiles)):
            for e in range(N_EXPERTS):
                window_copy(t, e, t).start()

    @pl.when(i + WIN_BUFFERS - 1 < ntiles)
    def _():
        for e in range(N_EXPERTS):
            window_copy(i + WIN_BUFFERS - 1, e, (i + WIN_BUFFERS - 1) % WIN_BUFFERS).start()

    slot = lax.broadcasted_iota(jnp.int32, (WIN_ROWS, TOK_TILE), 0)

    def one_hot(p, lo):
        st = jnp.minimum(lo, cap - WIN_ROWS)
        hit = (p - st == slot) & (p >= lo) & (p < lo + WIN_ROWS)
        return jnp.where(hit, 1.0, 0.0).astype(BF16)

    for e in range(N_EXPERTS):
        wt_ref[e * WIN_ROWS:(e + 1) * WIN_ROWS, :] = one_hot(pos_ref[e:e + 1, :], window_lo(i, e))
    for e in range(N_EXPERTS):
        window_copy(i, e, cur).wait()
    acc_ref[...] = lax.dot_general(wt_ref[...], ystack[cur], (((0,), (0,)), ((), ())),
                                   preferred_element_type=F32)

    @pl.when(extra_ref[i] > 0)
    def _():
        def per_expert(e, carry):
            lo = window_lo(i, e)
            nwin = (end_ref[e * ntiles + i] - lo + WIN_ROWS - 1) // WIN_ROWS
            p = pos_ref[pl.ds(e, 1), :]

            def per_window(w, c):
                lo_w = lo + w * WIN_ROWS
                st = pl.multiple_of(jnp.minimum(lo_w, cap - WIN_ROWS), WIN_ALIGN)
                cp = pltpu.make_async_copy(ye_hbm.at[e, pl.ds(st, WIN_ROWS), :], xbuf, xsem)
                cp.start()
                cp.wait()
                acc_ref[...] += lax.dot_general(one_hot(p, lo_w), xbuf[...], (((0,), (0,)), ((), ())),
                                                preferred_element_type=F32)
                return c

            lax.fori_loop(1, nwin, per_window, 0)
            return carry

        lax.fori_loop(0, N_EXPERTS, per_expert, 0)

    o_ref[...] = x1_ref[...] + mod_ref[5:6, :] * acc_ref[...]


def _combine(base, end, pos, x1, mod, ye):
    e, n = pos.shape
    cap, d = ye.shape[1], ye.shape[2]
    ntiles = n // TOK_TILE
    lo = (base // WIN_ALIGN) * WIN_ALIGN
    extra = jnp.max(jnp.maximum((end - lo + WIN_ROWS - 1) // WIN_ROWS - 1, 0), axis=0).astype(jnp.int32)
    tiles_per_seq = SEQ // TOK_TILE
    return pl.pallas_call(
        functools.partial(_combine_kernel, cap=cap, ntiles=ntiles),
        grid_spec=pltpu.PrefetchScalarGridSpec(
            num_scalar_prefetch=3, grid=(ntiles,),
            in_specs=[
                pl.BlockSpec((e, TOK_TILE), lambda i, *_: (0, i)),
                pl.BlockSpec((TOK_TILE, d), lambda i, *_: (i, 0)),
                pl.BlockSpec((None, N_MOD, d), lambda i, *_: (i // tiles_per_seq, 0, 0)),
                pl.BlockSpec(memory_space=pl.ANY),
            ],
            out_specs=pl.BlockSpec((TOK_TILE, d), lambda i, *_: (i, 0)),
            scratch_shapes=[
                pltpu.VMEM((WIN_BUFFERS, e * WIN_ROWS, d), BF16),
                pltpu.VMEM((e * WIN_ROWS, TOK_TILE), BF16),
                pltpu.VMEM((TOK_TILE, d), F32),
                pltpu.VMEM((WIN_ROWS, d), BF16),
                pltpu.SemaphoreType.DMA((WIN_BUFFERS,)),
                pltpu.SemaphoreType.DMA(()),
            ]),
        out_shape=jax.ShapeDtypeStruct((n, d), F32),
        compiler_params=_params(("arbitrary",)),
        name="combine",
    )(base.reshape(-1), end.reshape(-1), extra, pos, x1, mod, ye)


FFN_ROWS = 2048
FFN_SUB = 512
FFN_FT = 512


def _ffn_kernel(idx_ref, idx_next_ref, h_hbm, wg_ref, wu_ref, wd_ref, gate_ref, o_ref,
                xbuf, xb_ref, acc_ref, sem, *, nblk, nf):
    f = pl.program_id(2)
    blk = pl.program_id(0) * pl.num_programs(1) + pl.program_id(1)
    rows_blk = xb_ref.shape[0]
    per_step = rows_blk // nf

    def row_copy(idx, k, j):
        return pltpu.make_async_copy(h_hbm.at[pl.ds(idx[k * per_step + j], 1), :],
                                     xbuf.at[k, pl.ds(j, 1), :], sem)

    def wait_block():
        for k in range(nf):
            pltpu.make_async_copy(h_hbm.at[pl.ds(0, per_step), :], xbuf.at[k], sem).wait()

    @pl.when((blk == 0) & (f == 0))
    def _():
        for k in range(nf):
            def issue(j, carry):
                row_copy(idx_ref, k, j).start()
                return carry
            lax.fori_loop(0, per_step, issue, 0)

    @pl.when(f == 0)
    def _():
        wait_block()
        for k in range(nf):
            xb_ref[k * per_step:(k + 1) * per_step, :] = xbuf[k].astype(BF16)
        acc_ref[...] = jnp.zeros_like(acc_ref)

    for j in range(per_step):
        row_copy(idx_next_ref, f, j).start()

    wg = wg_ref[...].astype(BF16)
    wu = wu_ref[...].astype(BF16)
    wd = wd_ref[...].astype(BF16)
    sub = min(FFN_SUB, rows_blk)
    for t in range(rows_blk // sub):
        rows = slice(t * sub, (t + 1) * sub)
        xt = xb_ref[rows, :]
        a = _dot(xt, wg)
        b = _dot(xt, wu)
        acc_ref[rows, :] += _dot((_silu(a) * b).astype(BF16), wd)

    @pl.when(f == nf - 1)
    def _():
        o_ref[...] = (acc_ref[...] * gate_ref[...]).astype(o_ref.dtype)

    @pl.when((blk == nblk - 1) & (f == nf - 1))
    def _():
        wait_block()


def _expert_ffn(idx, h, w_gate, w_up, w_down, gates):
    e, cap = gates.shape[0], gates.shape[1]
    d = h.shape[1]
    ff = w_gate.shape[2]
    rb = min(cap, FFN_ROWS)
    nrb = cap // rb
    nblk = e * nrb
    nf = ff // FFN_FT
    return pl.pallas_call(
        functools.partial(_ffn_kernel, nblk=nblk, nf=nf),
        grid=(e, nrb, nf),
        in_specs=[
            pl.BlockSpec((rb,), lambda i, r, f: (i * nrb + r,), memory_space=pltpu.SMEM),
            pl.BlockSpec((rb,), lambda i, r, f: (jnp.minimum(i * nrb + r + 1, nblk - 1),),
                         memory_space=pltpu.SMEM),
            pl.BlockSpec(memory_space=pl.ANY),
            pl.BlockSpec((None, d, FFN_FT), lambda i, r, f: (i, 0, f)),
            pl.BlockSpec((None, d, FFN_FT), lambda i, r, f: (i, 0, f)),
            pl.BlockSpec((None, FFN_FT, d), lambda i, r, f: (i, f, 0)),
            pl.BlockSpec((None, rb, 1), lambda i, r, f: (i, r, 0)),
        ],
        out_specs=pl.BlockSpec((None, rb, d), lambda i, r, f: (i, r, 0)),
        out_shape=jax.ShapeDtypeStruct((e, cap, d), BF16),
        scratch_shapes=[
            pltpu.VMEM((nf, rb // nf, d), F32),
            pltpu.VMEM((rb, d), BF16),
            pltpu.VMEM((rb, d), F32),
            pltpu.SemaphoreType.DMA(()),
        ],
        compiler_params=_params(("arbitrary", "arbitrary", "arbitrary")),
        name="expert_ffn",
    )(idx, idx, h, w_gate, w_up, w_down, gates)


def _moe(h2, logits, x1, mod, wg, wu, wd):
    n, d = h2.shape
    cap = max(1, EC_FACTOR * n // N_EXPERTS)
    lt = logits.reshape(N_EXPERTS, n // LANES, LANES)
    aff, pos, rank = _route(lt, cap)
    aff = aff.reshape(N_EXPERTS, n)
    pos = pos.reshape(N_EXPERTS, n)
    base = rank.reshape(N_EXPERTS, n)[:, ::TOK_TILE]
    end = jnp.concatenate([base[:, 1:], jnp.full((N_EXPERTS, 1), cap, jnp.int32)], axis=1)
    idx, gates = _invert(base, end, pos, aff, cap)
    ye = _expert_ffn(idx, h2, wg, wu, wd, gates[..., None])
    return _combine(base, end, pos, x1, mod, ye)


def _trunk(x, c, p):
    b = x.shape[0]
    n = b * SEQ
    mod = _ada(c, p["ada_w"], p["ada_b"])
    xf = x.reshape(n, D_MODEL)
    for i in range(2):
        g1 = p["norm1_g"][i].reshape(1, D_MODEL)
        g2 = p["norm2_g"][i].reshape(1, D_MODEL)
        if i == 0:
            qkv = _norm_qkv(xf, mod[i], g1, p["w_qkv"])
            m = _attention(qkv.reshape(b, SEQ, 3 * D_MODEL), p["q_gain"], p["k_gain"], p["bias"])
            w_mix = p["w_o"]
        else:
            v = _fnet_in(xf, mod[i], g1, p["w_in"], p["cs"])
            m = _seq_dft(v.reshape(b, SEQ, 2 * D_MODEL), p["ct"], p["st_neg"])
            w_mix = p["w_out"]
        x1, h2, logits = _mix_out(m.reshape(n, D_MODEL), w_mix, xf, mod[i], g2, p["w_router"][i])
        xf = _moe(h2, logits, x1, mod[i], p["w_gate"][i], p["w_up"][i], p["w_down"][i])
    return xf.reshape(b, SEQ, D_MODEL)


def _split_router(w):
    wt = jnp.swapaxes(w, 1, 2)
    hi = wt.astype(BF16)
    lo = (wt - hi.astype(F32)).astype(BF16)
    return jnp.concatenate([hi, lo], axis=1)


def kernel(x_prompt, x_sample, c_prompt, c_sample, norm1_g, norm2_g, ada_w, ada_b, na_w_qkv, na_q_g, na_k_g, na_rpb, na_w_o, fn_w_in, fn_w_out, moe_w_router, moe_w_gate, moe_w_up, moe_w_down):
    cs, ct, st_neg = _dft_tables()
    tile2 = lambda g: jnp.tile(g.reshape(1, HEAD_DIM), (1, HEADS_PER_BLOCK))
    p = {
        "norm1_g": norm1_g, "norm2_g": norm2_g, "ada_w": ada_w, "ada_b": ada_b,
        "w_qkv": na_w_qkv[0].astype(BF16),
        "q_gain": tile2(na_q_g[0]) * (HEAD_DIM ** -0.5),
        "k_gain": tile2(na_k_g[0]),
        "bias": _attn_bias_tables(na_rpb[0]),
        "w_o": na_w_o[0].astype(BF16),
        "w_in": fn_w_in[0].astype(BF16),
        "w_out": fn_w_out[0].astype(BF16),
        "cs": cs, "ct": ct, "st_neg": st_neg,
        "w_router": _split_router(moe_w_router),
        "w_gate": moe_w_gate, "w_up": moe_w_up, "w_down": moe_w_down,
    }
    return (_trunk(x_prompt, c_prompt, p), _trunk(x_sample, c_sample, p))
```

```python
import functools

import numpy as np
import jax
import jax.numpy as jnp
from jax import lax
from jax.experimental import pallas as pl
from jax.experimental.pallas import tpu as pltpu

D_MODEL = 1024
SEQ = 2048
GRID_W = 64
ROWS = SEQ // GRID_W
NA_HEADS = 16
HEAD_DIM = D_MODEL // NA_HEADS
WIN_R = 8
WIN_C = 16
FN_GROUPS = 4
FN_GROUP_DIM = D_MODEL // FN_GROUPS
N_EXPERTS = 16
EC_FACTOR = 2
EXPERT_FF = 2048
N_MOD = 6
EPS = 1e-6
NEG = -1e30

LANES = 128
KEY_ROWS = min(WIN_R, ROWS)
KEYS = KEY_ROWS * GRID_W
HEADS_PER_BLOCK = LANES // HEAD_DIM
N_ROW_OFFSETS = KEY_ROWS

VMEM_LIMIT = 52 * 1024 * 1024

F32 = jnp.float32
BF16 = jnp.bfloat16


def _params(sem):
    return pltpu.CompilerParams(dimension_semantics=sem, vmem_limit_bytes=VMEM_LIMIT)


def _dot(a, b):
    return jnp.dot(a, b, preferred_element_type=F32)


def _dot_nt(a, b):
    return lax.dot_general(a, b, (((1,), (1,)), ((), ())), preferred_element_type=F32)


def _dot_split(a, b):
    a_hi = a.astype(BF16)
    a_lo = (a - a_hi.astype(F32)).astype(BF16)
    b_hi = b.astype(BF16)
    b_lo = (b - b_hi.astype(F32)).astype(BF16)
    return _dot(a_hi, b_hi) + (_dot(a_hi, b_lo) + _dot(a_lo, b_hi))


def _silu(a):
    return a * (1.0 / (1.0 + jnp.exp(-a)))


def _rms_mod(x, gain, shift, scale):
    ms = jnp.mean(x * x, axis=-1, keepdims=True)
    y = x * lax.rsqrt(ms + EPS) * gain
    return y * (1.0 + scale) + shift


def _ada_kernel(c_ref, w_ref, b_ref, o_ref):
    o_ref[...] = _dot_split(_silu(c_ref[...]), w_ref[...]) + b_ref[...]


def _ada(c, ada_w, ada_b):
    depth, d, n6 = ada_w.shape
    b = c.shape[0]
    tn = 1536
    out = pl.pallas_call(
        _ada_kernel,
        grid=(depth, n6 // tn),
        in_specs=[
            pl.BlockSpec((b, d), lambda i, j: (0, 0)),
            pl.BlockSpec((None, d, tn), lambda i, j: (i, 0, j)),
            pl.BlockSpec((None, 1, tn), lambda i, j: (i, 0, j)),
        ],
        out_specs=pl.BlockSpec((None, b, tn), lambda i, j: (i, 0, j)),
        out_shape=jax.ShapeDtypeStruct((depth, b, n6), F32),
        compiler_params=_params(("arbitrary", "arbitrary")),
        name="ada_mod",
    )(c, ada_w, ada_b.reshape(depth, 1, n6))
    return out.reshape(depth, b, N_MOD, d)


ROW_TILE = 512


def _qkv_kernel(x_ref, mod_ref, g_ref, w_ref, o_ref):
    h = _rms_mod(x_ref[...], g_ref[...], mod_ref[0:1, :], mod_ref[1:2, :])
    o_ref[...] = _dot(h.astype(BF16), w_ref[...]).astype(BF16)


def _norm_qkv(x, mod, gain, w):
    n, d = x.shape
    nn = w.shape[1]
    tiles_per_seq = SEQ // ROW_TILE
    return pl.pallas_call(
        _qkv_kernel,
        grid=(n // ROW_TILE,),
        in_specs=[
            pl.BlockSpec((ROW_TILE, d), lambda i: (i, 0)),
            pl.BlockSpec((None, N_MOD, d), lambda i: (i // tiles_per_seq, 0, 0)),
            pl.BlockSpec((1, d), lambda i: (0, 0)),
            pl.BlockSpec((d, nn), lambda i: (0, 0)),
        ],
        out_specs=pl.BlockSpec((ROW_TILE, nn), lambda i: (i, 0)),
        out_shape=jax.ShapeDtypeStruct((n, nn), BF16),
        compiler_params=_params(("arbitrary",)),
        name="norm_qkv",
    )(x, mod, gain, w)


def _fnet_in_kernel(x_ref, mod_ref, g_ref, w_ref, cs_ref, o_ref):
    h = _rms_mod(x_ref[...], g_ref[...], mod_ref[0:1, :], mod_ref[1:2, :])
    u = _dot(h.astype(BF16), w_ref[...]).astype(BF16)
    gd = FN_GROUP_DIM
    for g in range(FN_GROUPS):
        v = _dot(u[:, g * gd:(g + 1) * gd], cs_ref[...])
        o_ref[:, g * gd:(g + 1) * gd] = v[:, :gd].astype(BF16)
        o_ref[:, D_MODEL + g * gd:D_MODEL + (g + 1) * gd] = v[:, gd:].astype(BF16)


def _fnet_in(x, mod, gain, w, cs):
    n, d = x.shape
    tiles_per_seq = SEQ // ROW_TILE
    return pl.pallas_call(
        _fnet_in_kernel,
        grid=(n // ROW_TILE,),
        in_specs=[
            pl.BlockSpec((ROW_TILE, d), lambda i: (i, 0)),
            pl.BlockSpec((None, N_MOD, d), lambda i: (i // tiles_per_seq, 0, 0)),
            pl.BlockSpec((1, d), lambda i: (0, 0)),
            pl.BlockSpec((d, d), lambda i: (0, 0)),
            pl.BlockSpec(cs.shape, lambda i: (0, 0)),
        ],
        out_specs=pl.BlockSpec((ROW_TILE, 2 * d), lambda i: (i, 0)),
        out_shape=jax.ShapeDtypeStruct((n, 2 * d), BF16),
        compiler_params=_params(("arbitrary",)),
        name="fnet_in",
    )(x, mod, gain, w, cs)


SEQ_DFT_TILE = 1024
SEQ_DFT_SUB = 256


def _seq_dft_kernel(ct_ref, st_ref, v_ref, o_ref):
    scale = 1.0 / np.sqrt(SEQ)
    for s in range(SEQ_DFT_TILE // SEQ_DFT_SUB):
        rows = slice(s * SEQ_DFT_SUB, (s + 1) * SEQ_DFT_SUB)
        acc = _dot(ct_ref[rows, :], v_ref[:, :D_MODEL]) + _dot(st_ref[rows, :], v_ref[:, D_MODEL:])
        o_ref[rows, :] = (acc * scale).astype(BF16)


def _seq_dft(v, ct, st_neg):
    b = v.shape[0]
    return pl.pallas_call(
        _seq_dft_kernel,
        grid=(SEQ // SEQ_DFT_TILE, b),
        in_specs=[
            pl.BlockSpec((SEQ_DFT_TILE, SEQ), lambda k, i: (k, 0)),
            pl.BlockSpec((SEQ_DFT_TILE, SEQ), lambda k, i: (k, 0)),
            pl.BlockSpec((None, SEQ, 2 * D_MODEL), lambda k, i: (i, 0, 0)),
        ],
        out_specs=pl.BlockSpec((None, SEQ_DFT_TILE, D_MODEL), lambda k, i: (i, k, 0)),
        out_shape=jax.ShapeDtypeStruct((b, SEQ, D_MODEL), BF16),
        compiler_params=_params(("arbitrary", "arbitrary")),
        name="seq_dft",
    )(ct, st_neg, v)


def _dft_tables():
    def cos_sin(n):
        k = lax.broadcasted_iota(jnp.int32, (n, n), 0)
        t = lax.broadcasted_iota(jnp.int32, (n, n), 1)
        ang = ((k * t) % n).astype(F32) * (2.0 * np.pi / n)
        return jnp.cos(ang), jnp.sin(ang)
    cc, sc = cos_sin(FN_GROUP_DIM)
    cs = (jnp.concatenate([cc, sc], axis=1) * (1.0 / np.sqrt(FN_GROUP_DIM))).astype(BF16)
    ct, st = cos_sin(SEQ)
    return cs, ct.astype(BF16), (-st).astype(BF16)


def _bias_cols_kernel(rpb_ref, select_ref, valid_ref, o_ref):
    r = rpb_ref[...]
    hi = r.astype(BF16)
    r1 = r - hi.astype(F32)
    mid = r1.astype(BF16)
    low = (r1 - mid.astype(F32)).astype(BF16)
    sel = select_ref[...]
    t = _dot(hi, sel) + (_dot(mid, sel) + _dot(low, sel))
    o_ref[...] = jnp.where(valid_ref[...] > 0.0, t, NEG)


def _attn_bias_tables(rpb):
    h, nr, nc = rpb.shape
    q = np.arange(GRID_W)
    kc = np.arange(GRID_W)
    start = np.clip(q - WIN_C // 2, 0, GRID_W - WIN_C)
    rel = kc[None, :] - start[:, None]
    mask = (rel >= 0) & (rel < WIN_C)
    dcol = np.clip(kc[None, :] - q[:, None] + WIN_C - 1, 0, 2 * WIN_C - 2)
    pairs = GRID_W * GRID_W
    select = np.zeros((LANES, pairs), np.float32)
    select[dcol.reshape(-1), np.arange(pairs)] = 1.0
    cols = pl.pallas_call(
        _bias_cols_kernel,
        out_shape=jax.ShapeDtypeStruct((h * nr, pairs), F32),
        name="bias_cols",
    )(jnp.pad(rpb.reshape(h * nr, nc).astype(F32), ((0, 0), (0, LANES - nc))),
      jnp.asarray(select, BF16), jnp.asarray(mask.reshape(1, pairs), F32))
    cols = cols.reshape(h, nr, GRID_W, GRID_W)
    tbl = jnp.stack([cols[:, WIN_R - 1 - o:WIN_R - 1 - o + KEY_ROWS] for o in range(N_ROW_OFFSETS)], axis=1)
    tbl = tbl.transpose(0, 1, 3, 2, 4)
    return tbl.reshape(NA_HEADS // HEADS_PER_BLOCK, HEADS_PER_BLOCK, N_ROW_OFFSETS, GRID_W, KEYS)


ATTN_GROUP = 8


def _attn_kernel(q_ref, k_ref, v_ref, qg_ref, kg_ref, bias_ref, o_ref, qn_ref, kn_ref, s_ref, p_ref):
    lane = lax.broadcasted_iota(jnp.int32, (1, LANES), 1)
    lo = lane < HEAD_DIM

    ia = lax.broadcasted_iota(jnp.int32, (LANES, LANES), 0) // HEAD_DIM
    ib = lax.broadcasted_iota(jnp.int32, (LANES, LANES), 1) // HEAD_DIM
    same_head = jnp.where(ia == ib, 1.0, 0.0).astype(BF16)

    def head_norm(x, g):
        ms = _dot((x * x).astype(BF16), same_head) * (1.0 / HEAD_DIM)
        return x * lax.rsqrt(ms + EPS) * g

    qn_ref[...] = head_norm(q_ref[...].astype(F32), qg_ref[...]).astype(BF16)
    kn_ref[...] = head_norm(k_ref[...].astype(F32), kg_ref[...]).astype(BF16)

    def group(g, carry):
        starts = []
        for j in range(ATTN_GROUP):
            r = g * ATTN_GROUP + j
            rs = min(max(r - KEY_ROWS // 2, 0), ROWS - KEY_ROWS)
            off = r - rs
            k0 = rs * GRID_W
            q0 = r * GRID_W
            starts.append((k0, q0))
            kr = kn_ref[pl.ds(k0, KEYS), :]
            qr = qn_ref[pl.ds(q0, GRID_W), :]
            for h in range(HEADS_PER_BLOCK):
                keep = lo if h == 0 else jnp.logical_not(lo)
                qm = jnp.where(keep, qr, jnp.zeros_like(qr))
                s_ref[j * HEADS_PER_BLOCK + h] = _dot_nt(qm, kr) + bias_ref[h, off]
        inv = []
        for u in range(ATTN_GROUP * HEADS_PER_BLOCK):
            s = s_ref[u]
            p = jnp.exp(s - jnp.max(s, axis=-1, keepdims=True))
            inv.append(1.0 / jnp.sum(p, axis=-1, keepdims=True))
            p_ref[u] = p.astype(BF16)
        for j in range(ATTN_GROUP):
            k0, q0 = starts[j]
            vr = v_ref[pl.ds(k0, KEYS), :]
            outs = [_dot(p_ref[j * HEADS_PER_BLOCK + h], vr) * inv[j * HEADS_PER_BLOCK + h]
                    for h in range(HEADS_PER_BLOCK)]
            o_ref[pl.ds(q0, GRID_W), :] = jnp.where(lo, outs[0], outs[1]).astype(BF16)
        return carry

    for g in range(ROWS // ATTN_GROUP):
        group(g, 0)


def _attention(qkv, q_gain, k_gain, bias):
    b = qkv.shape[0]
    nblk = D_MODEL // LANES
    blk = lambda base: pl.BlockSpec((None, SEQ, LANES), lambda hp, i: (i, 0, base + hp))
    return pl.pallas_call(
        _attn_kernel,
        grid=(nblk, b),
        in_specs=[
            blk(0), blk(nblk), blk(2 * nblk),
            pl.BlockSpec((1, LANES), lambda hp, i: (0, 0)),
            pl.BlockSpec((1, LANES), lambda hp, i: (0, 0)),
            pl.BlockSpec((None, HEADS_PER_BLOCK, N_ROW_OFFSETS, GRID_W, KEYS),
                         lambda hp, i: (hp, 0, 0, 0, 0)),
        ],
        out_specs=pl.BlockSpec((None, SEQ, LANES), lambda hp, i: (i, 0, hp)),
        out_shape=jax.ShapeDtypeStruct((b, SEQ, D_MODEL), BF16),
        scratch_shapes=[
            pltpu.VMEM((SEQ, LANES), BF16), pltpu.VMEM((SEQ, LANES), BF16),
            pltpu.VMEM((ATTN_GROUP * HEADS_PER_BLOCK, GRID_W, KEYS), F32),
            pltpu.VMEM((ATTN_GROUP * HEADS_PER_BLOCK, GRID_W, KEYS), BF16),
        ],
        compiler_params=_params(("arbitrary", "arbitrary")),
        name="nbr_attention",
    )(qkv, qkv, qkv, q_gain, k_gain, bias)


def _mix_out_kernel(m_ref, w_ref, x_ref, mod_ref, g_ref, wr_ref, x1_ref, h2_ref, lg_ref):
    x1 = x_ref[...] + mod_ref[2:3, :] * _dot(m_ref[...], w_ref[...])
    x1_ref[...] = x1
    h2 = _rms_mod(x1, g_ref[...], mod_ref[3:4, :], mod_ref[4:5, :])
    h2_ref[...] = h2
    h_hi = h2.astype(BF16)
    h_lo = (h2 - h_hi.astype(F32)).astype(BF16)
    a = _dot_nt(wr_ref[...], h_hi)
    b = _dot_nt(wr_ref[0:N_EXPERTS, :], h_lo)
    lg_ref[...] = a[0:N_EXPERTS] + (a[N_EXPERTS:] + b)


def _mix_out(m, w, x, mod, gain, w_router):
    n, d = x.shape
    tiles_per_seq = SEQ // ROW_TILE
    row = lambda i: (i, 0)
    fixed = lambda i: (0, 0)
    return pl.pallas_call(
        _mix_out_kernel,
        grid=(n // ROW_TILE,),
        in_specs=[
            pl.BlockSpec((ROW_TILE, d), row),
            pl.BlockSpec((d, d), fixed),
            pl.BlockSpec((ROW_TILE, d), row),
            pl.BlockSpec((None, N_MOD, d), lambda i: (i // tiles_per_seq, 0, 0)),
            pl.BlockSpec((1, d), fixed),
            pl.BlockSpec((2 * N_EXPERTS, d), fixed),
        ],
        out_specs=[
            pl.BlockSpec((ROW_TILE, d), row),
            pl.BlockSpec((ROW_TILE, d), row),
            pl.BlockSpec((N_EXPERTS, ROW_TILE), lambda i: (0, i)),
        ],
        out_shape=[
            jax.ShapeDtypeStruct((n, d), F32),
            jax.ShapeDtypeStruct((n, d), F32),
            jax.ShapeDtypeStruct((N_EXPERTS, n), F32),
        ],
        compiler_params=_params(("arbitrary",)),
        name="mix_out",
    )(m, w, x, mod, gain, w_router)


def _route_kernel(lg_ref, aff_ref, pos_ref, rank_ref, *, cap):
    lg = lg_ref[...]
    rows = lg.shape[1]
    mx = jnp.max(lg, axis=0, keepdims=True)
    ex = jnp.exp(lg - mx)
    aff = ex / jnp.sum(ex, axis=0, keepdims=True)
    aff_ref[...] = aff

    def count(mask):
        c = jnp.sum(mask.astype(jnp.int32), axis=2, keepdims=True)
        return jnp.sum(c, axis=1, keepdims=True)

    def as_float(bits):
        return lax.bitcast_convert_type(bits, F32)

    def search(i, thr):
        cand = thr | jnp.left_shift(jnp.int32(1), 30 - i)
        return jnp.where(count(aff >= as_float(cand)) >= cap, cand, thr)

    thr = as_float(lax.fori_loop(0, 31, search, jnp.zeros((N_EXPERTS, 1, 1), jnp.int32)))
    gt = aff > thr
    eq = aff == thr
    need = cap - count(gt)

    a = lax.broadcasted_iota(jnp.int32, (LANES, LANES), 0)
    b = lax.broadcasted_iota(jnp.int32, (LANES, LANES), 1)
    upper = (a <= b).astype(BF16)
    ones = jnp.ones((LANES, LANES), BF16)
    ra = lax.broadcasted_iota(jnp.int32, (rows, rows), 0)
    rb = lax.broadcasted_iota(jnp.int32, (rows, rows), 1)
    lower = (rb < ra).astype(BF16)

    def excl_cumsum(mask):
        mb = mask.astype(BF16)
        within = _dot(mb, upper)
        before = _dot(lower, _dot(mb, ones).astype(BF16))
        return (within + before).astype(jnp.int32) - mask.astype(jnp.int32)

    for e in range(N_EXPERTS):
        sel = gt[e] | (eq[e] & (excl_cumsum(eq[e]) < need[e]))
        rank = excl_cumsum(sel)
        rank_ref[e] = rank
        pos_ref[e] = jnp.where(sel, rank, -1)


def _route(logits_t, cap):
    shape = logits_t.shape
    spec = pl.BlockSpec(shape, lambda: (0, 0, 0))
    return pl.pallas_call(
        functools.partial(_route_kernel, cap=cap),
        in_specs=[spec],
        out_specs=[spec, spec, spec],
        out_shape=[jax.ShapeDtypeStruct(shape, F32), jax.ShapeDtypeStruct(shape, jnp.int32),
                   jax.ShapeDtypeStruct(shape, jnp.int32)],
        compiler_params=pltpu.CompilerParams(vmem_limit_bytes=VMEM_LIMIT),
        name="route",
    )(logits_t)


TOK_TILE = 256


def _invert_kernel(base_ref, end_ref, extra_ref, pos_ref, aff_ref, idx_ref, gate_ref, *, ntiles):
    i = pl.program_id(0)
    nchunks = idx_ref.shape[1]

    @pl.when(i == 0)
    def _():
        idx_ref[...] = jnp.zeros_like(idx_ref)
        gate_ref[...] = jnp.zeros_like(gate_ref)

    local = lax.broadcasted_iota(jnp.int32, (1, TOK_TILE), 1).astype(F32)
    slot = lax.broadcasted_iota(jnp.int32, (LANES, TOK_TILE), 0)
    tile_start = (i * TOK_TILE).astype(F32)
    zero = jnp.zeros((11, TOK_TILE), BF16)

    def lhs_rows(g):
        g_hi = g.astype(BF16)
        r1 = g - g_hi.astype(F32)
        g_mid = r1.astype(BF16)
        g_lo = (r1 - g_mid.astype(F32)).astype(BF16)
        return jnp.concatenate([local.astype(BF16), jnp.ones((1, TOK_TILE), BF16), g_hi, g_mid, g_lo, zero], axis=0)

    def add_chunk(e, c, p, lhs):
        hit = jnp.where(p - c * LANES == slot, 1.0, 0.0).astype(BF16)
        r = _dot_nt(lhs, hit)
        row = pl.ds(jnp.minimum(c, nchunks - 1), 1)
        idx_ref[e, row, :] += r[0:1] + tile_start * r[1:2]
        gate_ref[e, row, :] += r[2:3] + r[3:4] + r[4:5]

    for e in range(N_EXPERTS):
        p = pos_ref[e:e + 1, :]
        lhs = lhs_rows(aff_ref[e:e + 1, :])
        first = base_ref[e * ntiles + i] // LANES
        add_chunk(e, first, p, lhs)
        add_chunk(e, first + 1, p, lhs)

    @pl.when(extra_ref[i] > 0)
    def _():
        def per_expert(e, carry):
            p = pos_ref[pl.ds(e, 1), :]
            lhs = lhs_rows(aff_ref[pl.ds(e, 1), :])
            first = base_ref[e * ntiles + i] // LANES
            last = (end_ref[e * ntiles + i] + LANES - 1) // LANES

            def per_chunk(c, c2):
                add_chunk(e, c, p, lhs)
                return c2

            lax.fori_loop(first + 2, last, per_chunk, 0)
            return carry

        lax.fori_loop(0, N_EXPERTS, per_expert, 0)


def _invert(base, end, pos, aff, cap):
    e, n = pos.shape
    ntiles = n // TOK_TILE
    nchunks_tile = (end + LANES - 1) // LANES - base // LANES
    extra = jnp.max(jnp.maximum(nchunks_tile - 2, 0), axis=0).astype(jnp.int32)
    out_spec = pl.BlockSpec((e, cap // LANES, LANES), lambda i, *_: (0, 0, 0))
    tok_spec = pl.BlockSpec((e, TOK_TILE), lambda i, *_: (0, i))
    idx, gates = pl.pallas_call(
        functools.partial(_invert_kernel, ntiles=ntiles),
        grid_spec=pltpu.PrefetchScalarGridSpec(
            num_scalar_prefetch=3, grid=(ntiles,),
            in_specs=[tok_spec, tok_spec], out_specs=[out_spec, out_spec]),
        out_shape=[jax.ShapeDtypeStruct((e, cap // LANES, LANES), F32)] * 2,
        compiler_params=_params(("arbitrary",)),
        name="slot_lists",
    )(base.reshape(-1), end.reshape(-1), extra, pos, aff)
    return idx.astype(jnp.int32).reshape(e * cap), gates.reshape(e, cap)


WIN_ROWS = 64
WIN_ALIGN = 16
WIN_BUFFERS = 3


def _combine_kernel(base_ref, end_ref, extra_ref, pos_ref, x1_ref, mod_ref, ye_hbm, o_ref,
                    ystack, wt_ref, xbuf, sems, xsem, *, cap, ntiles):
    i = pl.program_id(0)
    cur = i % WIN_BUFFERS

    def window_lo(tile, e):
        return (base_ref[e * ntiles + tile] // WIN_ALIGN) * WIN_ALIGN

    def window_copy(tile, e, buf):
        st = pl.multiple_of(jnp.minimum(window_lo(tile, e), cap - WIN_ROWS), WIN_ALIGN)
        return pltpu.make_async_copy(ye_hbm.at[e, pl.ds(st, WIN_ROWS), :],
                                     ystack.at[buf, pl.ds(e * WIN_ROWS, WIN_ROWS), :], sems.at[buf])

    @pl.when(i == 0)
    def _():
        for t in range(min(WIN_BUFFERS - 1, ntiles)):
            for e in range(N_EXPERTS):
                window_copy(t, e, t).start()

    @pl.when(i + WIN_BUFFERS - 1 < ntiles)
    def _():
        for e in range(N_EXPERTS):
            window_copy(i + WIN_BUFFERS - 1, e, (i + WIN_BUFFERS - 1) % WIN_BUFFERS).start()

    slot = lax.broadcasted_iota(jnp.int32, (WIN_ROWS, TOK_TILE), 0)

    def one_hot(p, lo):
        st = jnp.minimum(lo, cap - WIN_ROWS)
        hit = (p - st == slot) & (p >= lo) & (p < lo + WIN_ROWS)
        return jnp.where(hit, 1.0, 0.0).astype(BF16)

    for e in range(N_EXPERTS):
        wt_ref[e * WIN_ROWS:(e + 1) * WIN_ROWS, :] = one_hot(pos_ref[e:e + 1, :], window_lo(i, e))
    for e in range(N_EXPERTS):
        window_copy(i, e, cur).wait()
    gate2 = mod_ref[5:6, :]
    o_ref[...] = x1_ref[...] + gate2 * lax.dot_general(
        wt_ref[...], ystack[cur], (((0,), (0,)), ((), ())), preferred_element_type=F32)

    @pl.when(extra_ref[i] > 0)
    def _():
        def per_expert(e, carry):
            lo = window_lo(i, e)
            nwin = (end_ref[e * ntiles + i] - lo + WIN_ROWS - 1) // WIN_ROWS
            p = pos_ref[pl.ds(e, 1), :]

            def per_window(w, c):
                lo_w = lo + w * WIN_ROWS
                st = pl.multiple_of(jnp.minimum(lo_w, cap - WIN_ROWS), WIN_ALIGN)
                cp = pltpu.make_async_copy(ye_hbm.at[e, pl.ds(st, WIN_ROWS), :], xbuf, xsem)
                cp.start()
                cp.wait()
                o_ref[...] += gate2 * lax.dot_general(one_hot(p, lo_w), xbuf[...], (((0,), (0,)), ((), ())),
                                                      preferred_element_type=F32)
                return c

            lax.fori_loop(1, nwin, per_window, 0)
            return carry

        lax.fori_loop(0, N_EXPERTS, per_expert, 0)


def _combine(base, end, pos, x1, mod, ye):
    e, n = pos.shape
    cap, d = ye.shape[1], ye.shape[2]
    ntiles = n // TOK_TILE
    lo = (base // WIN_ALIGN) * WIN_ALIGN
    extra = jnp.max(jnp.maximum((end - lo + WIN_ROWS - 1) // WIN_ROWS - 1, 0), axis=0).astype(jnp.int32)
    tiles_per_seq = SEQ // TOK_TILE
    return pl.pallas_call(
        functools.partial(_combine_kernel, cap=cap, ntiles=ntiles),
        grid_spec=pltpu.PrefetchScalarGridSpec(
            num_scalar_prefetch=3, grid=(ntiles,),
            in_specs=[
                pl.BlockSpec((e, TOK_TILE), lambda i, *_: (0, i)),
                pl.BlockSpec((TOK_TILE, d), lambda i, *_: (i, 0)),
                pl.BlockSpec((None, N_MOD, d), lambda i, *_: (i // tiles_per_seq, 0, 0)),
                pl.BlockSpec(memory_space=pl.ANY),
            ],
            out_specs=pl.BlockSpec((TOK_TILE, d), lambda i, *_: (i, 0)),
            scratch_shapes=[
                pltpu.VMEM((WIN_BUFFERS, e * WIN_ROWS, d), BF16),
                pltpu.VMEM((e * WIN_ROWS, TOK_TILE), BF16),
                pltpu.VMEM((WIN_ROWS, d), BF16),
                pltpu.SemaphoreType.DMA((WIN_BUFFERS,)),
                pltpu.SemaphoreType.DMA(()),
            ]),
        out_shape=jax.ShapeDtypeStruct((n, d), F32),
        compiler_params=_params(("arbitrary",)),
        name="combine",
    )(base.reshape(-1), end.reshape(-1), extra, pos, x1, mod, ye)


FFN_ROWS = 2048
FFN_SUB = 512
FFN_FT = 512


def _ffn_kernel(idx_ref, idx_next_ref, h_hbm, wg_ref, wu_ref, wd_ref, gate_ref, o_ref,
                xbuf, xb_ref, acc_ref, sem, *, nblk, nf):
    f = pl.program_id(2)
    blk = pl.program_id(0) * pl.num_programs(1) + pl.program_id(1)
    rows_blk = xb_ref.shape[0]
    per_step = rows_blk // nf

    def row_copy(idx, k, j):
        return pltpu.make_async_copy(h_hbm.at[pl.ds(idx[k * per_step + j], 1), :],
                                     xbuf.at[k, pl.ds(j, 1), :], sem)

    def wait_block():
        for k in range(nf):
            pltpu.make_async_copy(h_hbm.at[pl.ds(0, per_step), :], xbuf.at[k], sem).wait()

    @pl.when((blk == 0) & (f == 0))
    def _():
        for k in range(nf):
            def issue(j, carry):
                row_copy(idx_ref, k, j).start()
                return carry
            lax.fori_loop(0, per_step, issue, 0)

    @pl.when(f == 0)
    def _():
        wait_block()
        for k in range(nf):
            xb_ref[k * per_step:(k + 1) * per_step, :] = xbuf[k].astype(BF16)
        acc_ref[...] = jnp.zeros_like(acc_ref)

    for j in range(per_step):
        row_copy(idx_next_ref, f, j).start()

    wg = wg_ref[...].astype(BF16)
    wu = wu_ref[...].astype(BF16)
    wd = wd_ref[...].astype(BF16)
    sub = min(FFN_SUB, rows_blk)
    for t in range(rows_blk // sub):
        rows = slice(t * sub, (t + 1) * sub)
        xt = xb_ref[rows, :]
        a = _dot(xt, wg)
        b = _dot(xt, wu)
        acc_ref[rows, :] += _dot((_silu(a) * b).astype(BF16), wd)

    @pl.when(f == nf - 1)
    def _():
        o_ref[...] = (acc_ref[...] * gate_ref[...]).astype(o_ref.dtype)

    @pl.when((blk == nblk - 1) & (f == nf - 1))
    def _():
        wait_block()


def _expert_ffn(idx, h, w_gate, w_up, w_down, gates, layer):
    e, cap = gates.shape[0], gates.shape[1]
    d = h.shape[1]
    ff = w_gate.shape[3]
    rb = min(cap, FFN_ROWS)
    nrb = cap // rb
    nblk = e * nrb
    nf = ff // FFN_FT
    return pl.pallas_call(
        functools.partial(_ffn_kernel, nblk=nblk, nf=nf),
        grid=(e, nrb, nf),
        in_specs=[
            pl.BlockSpec((rb,), lambda i, r, f: (i * nrb + r,), memory_space=pltpu.SMEM),
            pl.BlockSpec((rb,), lambda i, r, f: (jnp.minimum(i * nrb + r + 1, nblk - 1),),
                         memory_space=pltpu.SMEM),
            pl.BlockSpec(memory_space=pl.ANY),
            pl.BlockSpec((None, None, d, FFN_FT), lambda i, r, f: (layer, i, 0, f)),
            pl.BlockSpec((None, None, d, FFN_FT), lambda i, r, f: (layer, i, 0, f)),
            pl.BlockSpec((None, None, FFN_FT, d), lambda i, r, f: (layer, i, f, 0)),
            pl.BlockSpec((None, rb, 1), lambda i, r, f: (i, r, 0)),
        ],
        out_specs=pl.BlockSpec((None, rb, d), lambda i, r, f: (i, r, 0)),
        out_shape=jax.ShapeDtypeStruct((e, cap, d), BF16),
        scratch_shapes=[
            pltpu.VMEM((nf, rb // nf, d), F32),
            pltpu.VMEM((rb, d), BF16),
            pltpu.VMEM((rb, d), F32),
            pltpu.SemaphoreType.DMA(()),
        ],
        compiler_params=_params(("arbitrary", "arbitrary", "arbitrary")),
        name="expert_ffn",
    )(idx, idx, h, w_gate, w_up, w_down, gates)


def _moe(h2, logits, x1, mod, wg, wu, wd, layer):
    n, d = h2.shape
    cap = max(1, EC_FACTOR * n // N_EXPERTS)
    lt = logits.reshape(N_EXPERTS, n // LANES, LANES)
    aff, pos, rank = _route(lt, cap)
    aff = aff.reshape(N_EXPERTS, n)
    pos = pos.reshape(N_EXPERTS, n)
    base = rank.reshape(N_EXPERTS, n)[:, ::TOK_TILE]
    end = jnp.concatenate([base[:, 1:], jnp.full((N_EXPERTS, 1), cap, jnp.int32)], axis=1)
    idx, gates = _invert(base, end, pos, aff, cap)
    ye = _expert_ffn(idx, h2, wg, wu, wd, gates[..., None], layer)
    return _combine(base, end, pos, x1, mod, ye)


def _trunk(x, c, p):
    b = x.shape[0]
    n = b * SEQ
    mod = _ada(c, p["ada_w"], p["ada_b"])
    xf = x.reshape(n, D_MODEL)
    for i in range(2):
        g1 = p["norm1_g"][i].reshape(1, D_MODEL)
        g2 = p["norm2_g"][i].reshape(1, D_MODEL)
        if i == 0:
            qkv = _norm_qkv(xf, mod[i], g1, p["w_qkv"])
            m = _attention(qkv.reshape(b, SEQ, 3 * D_MODEL), p["q_gain"], p["k_gain"], p["bias"])
            w_mix = p["w_o"]
        else:
            v = _fnet_in(xf, mod[i], g1, p["w_in"], p["cs"])
            m = _seq_dft(v.reshape(b, SEQ, 2 * D_MODEL), p["ct"], p["st_neg"])
            w_mix = p["w_out"]
        x1, h2, logits = _mix_out(m.reshape(n, D_MODEL), w_mix, xf, mod[i], g2, p["w_router"][i])
        xf = _moe(h2, logits, x1, mod[i], p["w_gate"], p["w_up"], p["w_down"], i)
    return xf.reshape(b, SEQ, D_MODEL)


def _split_router(w):
    wt = jnp.swapaxes(w, 1, 2)
    hi = wt.astype(BF16)
    lo = (wt - hi.astype(F32)).astype(BF16)
    return jnp.concatenate([hi, lo], axis=1)


def kernel(x_prompt, x_sample, c_prompt, c_sample, norm1_g, norm2_g, ada_w, ada_b, na_w_qkv, na_q_g, na_k_g, na_rpb, na_w_o, fn_w_in, fn_w_out, moe_w_router, moe_w_gate, moe_w_up, moe_w_down):
    cs, ct, st_neg = _dft_tables()
    tile2 = lambda g: jnp.tile(g.reshape(1, HEAD_DIM), (1, HEADS_PER_BLOCK))
    p = {
        "norm1_g": norm1_g, "norm2_g": norm2_g, "ada_w": ada_w, "ada_b": ada_b,
        "w_qkv": na_w_qkv[0].astype(BF16),
        "q_gain": tile2(na_q_g[0]) * (HEAD_DIM ** -0.5),
        "k_gain": tile2(na_k_g[0]),
        "bias": _attn_bias_tables(na_rpb[0]),
        "w_o": na_w_o[0].astype(BF16),
        "w_in": fn_w_in[0].astype(BF16),
        "w_out": fn_w_out[0].astype(BF16),
        "cs": cs, "ct": ct, "st_neg": st_neg,
        "w_router": _split_router(moe_w_router),
        "w_gate": moe_w_gate, "w_up": moe_w_up, "w_down": moe_w_down,
    }
    return (_trunk(x_prompt, c_prompt, p), _trunk(x_sample, c_sample, p))
```

```python
import functools

import numpy as np
import jax
import jax.numpy as jnp
from jax import lax
from jax.experimental import pallas as pl
from jax.experimental.pallas import tpu as pltpu

D_MODEL = 1024
SEQ = 2048
GRID_W = 64
ROWS = SEQ // GRID_W
NA_HEADS = 16
HEAD_DIM = D_MODEL // NA_HEADS
WIN_R = 8
WIN_C = 16
FN_GROUPS = 4
FN_GROUP_DIM = D_MODEL // FN_GROUPS
N_EXPERTS = 16
EC_FACTOR = 2
EXPERT_FF = 2048
N_MOD = 6
EPS = 1e-6
NEG = -1e30

LANES = 128
KEY_ROWS = min(WIN_R, ROWS)
KEYS = KEY_ROWS * GRID_W
HEADS_PER_BLOCK = LANES // HEAD_DIM
N_ROW_OFFSETS = KEY_ROWS

VMEM_LIMIT = 52 * 1024 * 1024

F32 = jnp.float32
BF16 = jnp.bfloat16


def _params(sem):
    return pltpu.CompilerParams(dimension_semantics=sem, vmem_limit_bytes=VMEM_LIMIT)


def _dot(a, b):
    return jnp.dot(a, b, preferred_element_type=F32)


def _dot_nt(a, b):
    return lax.dot_general(a, b, (((1,), (1,)), ((), ())), preferred_element_type=F32)


def _dot_split(a, b):
    a_hi = a.astype(BF16)
    a_lo = (a - a_hi.astype(F32)).astype(BF16)
    b_hi = b.astype(BF16)
    b_lo = (b - b_hi.astype(F32)).astype(BF16)
    return _dot(a_hi, b_hi) + (_dot(a_hi, b_lo) + _dot(a_lo, b_hi))


def _silu(a):
    return a * (1.0 / (1.0 + jnp.exp(-a)))


def _rms_mod(x, gain, shift, scale):
    ms = jnp.mean(x * x, axis=-1, keepdims=True)
    y = x * lax.rsqrt(ms + EPS) * gain
    return y * (1.0 + scale) + shift


def _ada_kernel(c_ref, w_ref, b_ref, o_ref):
    o_ref[...] = _dot_split(_silu(c_ref[...]), w_ref[...]) + b_ref[...]


def _ada(c, ada_w, ada_b):
    depth, d, n6 = ada_w.shape
    b = c.shape[0]
    tn = 1536
    out = pl.pallas_call(
        _ada_kernel,
        grid=(depth, n6 // tn),
        in_specs=[
            pl.BlockSpec((b, d), lambda i, j: (0, 0)),
            pl.BlockSpec((None, d, tn), lambda i, j: (i, 0, j)),
            pl.BlockSpec((None, 1, tn), lambda i, j: (i, 0, j)),
        ],
        out_specs=pl.BlockSpec((None, b, tn), lambda i, j: (i, 0, j)),
        out_shape=jax.ShapeDtypeStruct((depth, b, n6), F32),
        compiler_params=_params(("arbitrary", "arbitrary")),
        name="ada_mod",
    )(c, ada_w, ada_b.reshape(depth, 1, n6))
    return out.reshape(depth, b, N_MOD, d)


ROW_TILE = 512


def _qkv_kernel(x_ref, mod_ref, g_ref, w_ref, o_ref):
    h = _rms_mod(x_ref[...], g_ref[...], mod_ref[0:1, :], mod_ref[1:2, :])
    o_ref[...] = _dot(h.astype(BF16), w_ref[...]).astype(BF16)


def _norm_qkv(x, mod, gain, w):
    n, d = x.shape
    nn = w.shape[1]
    tiles_per_seq = SEQ // ROW_TILE
    return pl.pallas_call(
        _qkv_kernel,
        grid=(n // ROW_TILE,),
        in_specs=[
            pl.BlockSpec((ROW_TILE, d), lambda i: (i, 0)),
            pl.BlockSpec((None, N_MOD, d), lambda i: (i // tiles_per_seq, 0, 0)),
            pl.BlockSpec((1, d), lambda i: (0, 0)),
            pl.BlockSpec((d, nn), lambda i: (0, 0)),
        ],
        out_specs=pl.BlockSpec((ROW_TILE, nn), lambda i: (i, 0)),
        out_shape=jax.ShapeDtypeStruct((n, nn), BF16),
        compiler_params=_params(("arbitrary",)),
        name="norm_qkv",
    )(x, mod, gain, w)


def _fnet_in_kernel(x_ref, mod_ref, g_ref, w_ref, cs_ref, o_ref):
    h = _rms_mod(x_ref[...], g_ref[...], mod_ref[0:1, :], mod_ref[1:2, :])
    u = _dot(h.astype(BF16), w_ref[...]).astype(BF16)
    gd = FN_GROUP_DIM
    for g in range(FN_GROUPS):
        v = _dot(u[:, g * gd:(g + 1) * gd], cs_ref[...])
        o_ref[:, g * gd:(g + 1) * gd] = v[:, :gd].astype(BF16)
        o_ref[:, D_MODEL + g * gd:D_MODEL + (g + 1) * gd] = v[:, gd:].astype(BF16)


def _fnet_in(x, mod, gain, w, cs):
    n, d = x.shape
    tiles_per_seq = SEQ // ROW_TILE
    return pl.pallas_call(
        _fnet_in_kernel,
        grid=(n // ROW_TILE,),
        in_specs=[
            pl.BlockSpec((ROW_TILE, d), lambda i: (i, 0)),
            pl.BlockSpec((None, N_MOD, d), lambda i: (i // tiles_per_seq, 0, 0)),
            pl.BlockSpec((1, d), lambda i: (0, 0)),
            pl.BlockSpec((d, d), lambda i: (0, 0)),
            pl.BlockSpec(cs.shape, lambda i: (0, 0)),
        ],
        out_specs=pl.BlockSpec((ROW_TILE, 2 * d), lambda i: (i, 0)),
        out_shape=jax.ShapeDtypeStruct((n, 2 * d), BF16),
        compiler_params=_params(("arbitrary",)),
        name="fnet_in",
    )(x, mod, gain, w, cs)


SEQ_DFT_TILE = 1024
SEQ_DFT_SUB = 256


def _seq_dft_kernel(ct_ref, st_ref, v_ref, o_ref):
    scale = 1.0 / np.sqrt(SEQ)
    for s in range(SEQ_DFT_TILE // SEQ_DFT_SUB):
        rows = slice(s * SEQ_DFT_SUB, (s + 1) * SEQ_DFT_SUB)
        acc = _dot(ct_ref[rows, :], v_ref[:, :D_MODEL]) + _dot(st_ref[rows, :], v_ref[:, D_MODEL:])
        o_ref[rows, :] = (acc * scale).astype(BF16)


def _seq_dft(v, ct, st_neg):
    b = v.shape[0]
    return pl.pallas_call(
        _seq_dft_kernel,
        grid=(SEQ // SEQ_DFT_TILE, b),
        in_specs=[
            pl.BlockSpec((SEQ_DFT_TILE, SEQ), lambda k, i: (k, 0)),
            pl.BlockSpec((SEQ_DFT_TILE, SEQ), lambda k, i: (k, 0)),
            pl.BlockSpec((None, SEQ, 2 * D_MODEL), lambda k, i: (i, 0, 0)),
        ],
        out_specs=pl.BlockSpec((None, SEQ_DFT_TILE, D_MODEL), lambda k, i: (i, k, 0)),
        out_shape=jax.ShapeDtypeStruct((b, SEQ, D_MODEL), BF16),
        compiler_params=_params(("arbitrary", "arbitrary")),
        name="seq_dft",
    )(ct, st_neg, v)


def _dft_tables():
    def cos_sin(n):
        k = lax.broadcasted_iota(jnp.int32, (n, n), 0)
        t = lax.broadcasted_iota(jnp.int32, (n, n), 1)
        ang = ((k * t) % n).astype(F32) * (2.0 * np.pi / n)
        return jnp.cos(ang), jnp.sin(ang)
    cc, sc = cos_sin(FN_GROUP_DIM)
    cs = (jnp.concatenate([cc, sc], axis=1) * (1.0 / np.sqrt(FN_GROUP_DIM))).astype(BF16)
    ct, st = cos_sin(SEQ)
    return cs, ct.astype(BF16), (-st).astype(BF16)


def _bias_cols_kernel(rpb_ref, select_ref, valid_ref, o_ref):
    r = rpb_ref[...]
    hi = r.astype(BF16)
    r1 = r - hi.astype(F32)
    mid = r1.astype(BF16)
    low = (r1 - mid.astype(F32)).astype(BF16)
    sel = select_ref[...]
    t = _dot(hi, sel) + (_dot(mid, sel) + _dot(low, sel))
    o_ref[...] = jnp.where(valid_ref[...] > 0.0, t, NEG)


def _attn_bias_tables(rpb):
    h, nr, nc = rpb.shape
    q = np.arange(GRID_W)
    kc = np.arange(GRID_W)
    start = np.clip(q - WIN_C // 2, 0, GRID_W - WIN_C)
    rel = kc[None, :] - start[:, None]
    mask = (rel >= 0) & (rel < WIN_C)
    dcol = np.clip(kc[None, :] - q[:, None] + WIN_C - 1, 0, 2 * WIN_C - 2)
    pairs = GRID_W * GRID_W
    select = np.zeros((LANES, pairs), np.float32)
    select[dcol.reshape(-1), np.arange(pairs)] = 1.0
    cols = pl.pallas_call(
        _bias_cols_kernel,
        out_shape=jax.ShapeDtypeStruct((h * nr, pairs), F32),
        name="bias_cols",
    )(jnp.pad(rpb.reshape(h * nr, nc).astype(F32), ((0, 0), (0, LANES - nc))),
      jnp.asarray(select, BF16), jnp.asarray(mask.reshape(1, pairs), F32))
    cols = cols.reshape(h, nr, GRID_W, GRID_W)
    tbl = jnp.stack([cols[:, WIN_R - 1 - o:WIN_R - 1 - o + KEY_ROWS] for o in range(N_ROW_OFFSETS)], axis=1)
    tbl = tbl.transpose(0, 1, 3, 2, 4)
    return tbl.reshape(NA_HEADS // HEADS_PER_BLOCK, HEADS_PER_BLOCK, N_ROW_OFFSETS, GRID_W, KEYS)


ATTN_GROUP = 8


def _attn_kernel(q_ref, k_ref, v_ref, qg_ref, kg_ref, bias_ref, o_ref, qn_ref, kn_ref, s_ref, p_ref):
    lane = lax.broadcasted_iota(jnp.int32, (1, LANES), 1)
    lo = lane < HEAD_DIM

    ia = lax.broadcasted_iota(jnp.int32, (LANES, LANES), 0) // HEAD_DIM
    ib = lax.broadcasted_iota(jnp.int32, (LANES, LANES), 1) // HEAD_DIM
    head_mean = jnp.where(ia == ib, 1.0 / HEAD_DIM, 0.0).astype(BF16)

    def head_norm(x, g):
        ms = _dot((x * x).astype(BF16), head_mean)
        return x * lax.rsqrt(ms + EPS) * g

    qn_ref[...] = head_norm(q_ref[...].astype(F32), qg_ref[...]).astype(BF16)
    kn_ref[...] = head_norm(k_ref[...].astype(F32), kg_ref[...]).astype(BF16)

    def group(g, carry):
        starts = []
        for j in range(ATTN_GROUP):
            r = g * ATTN_GROUP + j
            rs = min(max(r - KEY_ROWS // 2, 0), ROWS - KEY_ROWS)
            off = r - rs
            k0 = rs * GRID_W
            q0 = r * GRID_W
            starts.append((k0, q0))
            kr = kn_ref[pl.ds(k0, KEYS), :]
            qr = qn_ref[pl.ds(q0, GRID_W), :]
            for h in range(HEADS_PER_BLOCK):
                keep = lo if h == 0 else jnp.logical_not(lo)
                qm = jnp.where(keep, qr, jnp.zeros_like(qr))
                s_ref[j * HEADS_PER_BLOCK + h] = _dot_nt(qm, kr) + bias_ref[h, off]
        inv = []
        for u in range(ATTN_GROUP * HEADS_PER_BLOCK):
            s = s_ref[u]
            p = jnp.exp(s - jnp.max(s, axis=-1, keepdims=True))
            inv.append(1.0 / jnp.sum(p, axis=-1, keepdims=True))
            p_ref[u] = p.astype(BF16)
        for j in range(ATTN_GROUP):
            k0, q0 = starts[j]
            vr = v_ref[pl.ds(k0, KEYS), :]
            outs = [_dot(p_ref[j * HEADS_PER_BLOCK + h], vr) * inv[j * HEADS_PER_BLOCK + h]
                    for h in range(HEADS_PER_BLOCK)]
            o_ref[pl.ds(q0, GRID_W), :] = jnp.where(lo, outs[0], outs[1]).astype(BF16)
        return carry

    for g in range(ROWS // ATTN_GROUP):
        group(g, 0)


def _attention(qkv, q_gain, k_gain, bias):
    b = qkv.shape[0]
    nblk = D_MODEL // LANES
    blk = lambda base: pl.BlockSpec((None, SEQ, LANES), lambda hp, i: (i, 0, base + hp))
    return pl.pallas_call(
        _attn_kernel,
        grid=(nblk, b),
        in_specs=[
            blk(0), blk(nblk), blk(2 * nblk),
            pl.BlockSpec((1, LANES), lambda hp, i: (0, 0)),
            pl.BlockSpec((1, LANES), lambda hp, i: (0, 0)),
            pl.BlockSpec((None, HEADS_PER_BLOCK, N_ROW_OFFSETS, GRID_W, KEYS),
                         lambda hp, i: (hp, 0, 0, 0, 0)),
        ],
        out_specs=pl.BlockSpec((None, SEQ, LANES), lambda hp, i: (i, 0, hp)),
        out_shape=jax.ShapeDtypeStruct((b, SEQ, D_MODEL), BF16),
        scratch_shapes=[
            pltpu.VMEM((SEQ, LANES), BF16), pltpu.VMEM((SEQ, LANES), BF16),
            pltpu.VMEM((ATTN_GROUP * HEADS_PER_BLOCK, GRID_W, KEYS), F32),
            pltpu.VMEM((ATTN_GROUP * HEADS_PER_BLOCK, GRID_W, KEYS), BF16),
        ],
        compiler_params=_params(("arbitrary", "arbitrary")),
        name="nbr_attention",
    )(qkv, qkv, qkv, q_gain, k_gain, bias)


def _mix_out_kernel(m_ref, w_ref, x_ref, mod_ref, g_ref, wr_ref, x1_ref, h2_ref, lg_ref):
    x1 = x_ref[...] + mod_ref[2:3, :] * _dot(m_ref[...], w_ref[...])
    x1_ref[...] = x1
    h2 = _rms_mod(x1, g_ref[...], mod_ref[3:4, :], mod_ref[4:5, :])
    h2_ref[...] = h2
    h_hi = h2.astype(BF16)
    h_lo = (h2 - h_hi.astype(F32)).astype(BF16)
    a = _dot_nt(wr_ref[...], h_hi)
    b = _dot_nt(wr_ref[0:N_EXPERTS, :], h_lo)
    lg_ref[...] = a[0:N_EXPERTS] + (a[N_EXPERTS:] + b)


def _mix_out(m, w, x, mod, gain, w_router):
    n, d = x.shape
    tiles_per_seq = SEQ // ROW_TILE
    row = lambda i: (i, 0)
    fixed = lambda i: (0, 0)
    return pl.pallas_call(
        _mix_out_kernel,
        grid=(n // ROW_TILE,),
        in_specs=[
            pl.BlockSpec((ROW_TILE, d), row),
            pl.BlockSpec((d, d), fixed),
            pl.BlockSpec((ROW_TILE, d), row),
            pl.BlockSpec((None, N_MOD, d), lambda i: (i // tiles_per_seq, 0, 0)),
            pl.BlockSpec((1, d), fixed),
            pl.BlockSpec((2 * N_EXPERTS, d), fixed),
        ],
        out_specs=[
            pl.BlockSpec((ROW_TILE, d), row),
            pl.BlockSpec((ROW_TILE, d), row),
            pl.BlockSpec((N_EXPERTS, ROW_TILE), lambda i: (0, i)),
        ],
        out_shape=[
            jax.ShapeDtypeStruct((n, d), F32),
            jax.ShapeDtypeStruct((n, d), F32),
            jax.ShapeDtypeStruct((N_EXPERTS, n), F32),
        ],
        compiler_params=_params(("arbitrary",)),
        name="mix_out",
    )(m, w, x, mod, gain, w_router)


def _route_kernel(lg_ref, aff_ref, pos_ref, rank_ref, *, cap):
    lg = lg_ref[...]
    rows = lg.shape[1]
    mx = jnp.max(lg, axis=0, keepdims=True)
    ex = jnp.exp(lg - mx)
    aff = ex / jnp.sum(ex, axis=0, keepdims=True)
    aff_ref[...] = aff

    def count(mask):
        c = jnp.sum(mask.astype(jnp.int32), axis=2, keepdims=True)
        return jnp.sum(c, axis=1, keepdims=True)

    def as_float(bits):
        return lax.bitcast_convert_type(bits, F32)

    def search(i, thr):
        cand = thr | jnp.left_shift(jnp.int32(1), 30 - i)
        return jnp.where(count(aff >= as_float(cand)) >= cap, cand, thr)

    thr = as_float(lax.fori_loop(0, 31, search, jnp.zeros((N_EXPERTS, 1, 1), jnp.int32)))
    gt = aff > thr
    eq = aff == thr
    need = cap - count(gt)

    a = lax.broadcasted_iota(jnp.int32, (LANES, LANES), 0)
    b = lax.broadcasted_iota(jnp.int32, (LANES, LANES), 1)
    upper = (a <= b).astype(BF16)
    ones = jnp.ones((LANES, LANES), BF16)
    ra = lax.broadcasted_iota(jnp.int32, (rows, rows), 0)
    rb = lax.broadcasted_iota(jnp.int32, (rows, rows), 1)
    lower = (rb < ra).astype(BF16)

    def excl_cumsum(mask):
        mb = mask.astype(BF16)
        within = _dot(mb, upper)
        before = _dot(lower, _dot(mb, ones).astype(BF16))
        return (within + before).astype(jnp.int32) - mask.astype(jnp.int32)

    for e in range(N_EXPERTS):
        sel = gt[e] | (eq[e] & (excl_cumsum(eq[e]) < need[e]))
        rank = excl_cumsum(sel)
        rank_ref[e] = rank
        pos_ref[e] = jnp.where(sel, rank, -1)


def _route(logits_t, cap):
    shape = logits_t.shape
    spec = pl.BlockSpec(shape, lambda: (0, 0, 0))
    return pl.pallas_call(
        functools.partial(_route_kernel, cap=cap),
        in_specs=[spec],
        out_specs=[spec, spec, spec],
        out_shape=[jax.ShapeDtypeStruct(shape, F32), jax.ShapeDtypeStruct(shape, jnp.int32),
                   jax.ShapeDtypeStruct(shape, jnp.int32)],
        compiler_params=pltpu.CompilerParams(vmem_limit_bytes=VMEM_LIMIT),
        name="route",
    )(logits_t)


TOK_TILE = 256


def _invert_kernel(base_ref, end_ref, extra_ref, pos_ref, aff_ref, idx_ref, gate_ref, *, ntiles):
    i = pl.program_id(0)
    nchunks = idx_ref.shape[1]

    @pl.when(i == 0)
    def _():
        idx_ref[...] = jnp.zeros_like(idx_ref)
        gate_ref[...] = jnp.zeros_like(gate_ref)

    local = lax.broadcasted_iota(jnp.int32, (1, TOK_TILE), 1).astype(F32)
    slot = lax.broadcasted_iota(jnp.int32, (LANES, TOK_TILE), 0)
    tile_start = (i * TOK_TILE).astype(F32)
    zero = jnp.zeros((11, TOK_TILE), BF16)

    def lhs_rows(g):
        g_hi = g.astype(BF16)
        r1 = g - g_hi.astype(F32)
        g_mid = r1.astype(BF16)
        g_lo = (r1 - g_mid.astype(F32)).astype(BF16)
        return jnp.concatenate([local.astype(BF16), jnp.ones((1, TOK_TILE), BF16), g_hi, g_mid, g_lo, zero], axis=0)

    def add_chunk(e, c, p, lhs):
        hit = jnp.where(p - c * LANES == slot, 1.0, 0.0).astype(BF16)
        r = _dot_nt(lhs, hit)
        row = pl.ds(jnp.minimum(c, nchunks - 1), 1)
        idx_ref[e, row, :] += r[0:1] + tile_start * r[1:2]
        gate_ref[e, row, :] += r[2:3] + r[3:4] + r[4:5]

    for e in range(N_EXPERTS):
        p = pos_ref[e:e + 1, :]
        lhs = lhs_rows(aff_ref[e:e + 1, :])
        first = base_ref[e * ntiles + i] // LANES
        add_chunk(e, first, p, lhs)
        add_chunk(e, first + 1, p, lhs)

    @pl.when(extra_ref[i] > 0)
    def _():
        def per_expert(e, carry):
            p = pos_ref[pl.ds(e, 1), :]
            lhs = lhs_rows(aff_ref[pl.ds(e, 1), :])
            first = base_ref[e * ntiles + i] // LANES
            last = (end_ref[e * ntiles + i] + LANES - 1) // LANES

            def per_chunk(c, c2):
                add_chunk(e, c, p, lhs)
                return c2

            lax.fori_loop(first + 2, last, per_chunk, 0)
            return carry

        lax.fori_loop(0, N_EXPERTS, per_expert, 0)


def _invert(base, end, pos, aff, cap):
    e, n = pos.shape
    ntiles = n // TOK_TILE
    nchunks_tile = (end + LANES - 1) // LANES - base // LANES
    extra = jnp.max(jnp.maximum(nchunks_tile - 2, 0), axis=0).astype(jnp.int32)
    out_spec = pl.BlockSpec((e, cap // LANES, LANES), lambda i, *_: (0, 0, 0))
    tok_spec = pl.BlockSpec((e, TOK_TILE), lambda i, *_: (0, i))
    idx, gates = pl.pallas_call(
        functools.partial(_invert_kernel, ntiles=ntiles),
        grid_spec=pltpu.PrefetchScalarGridSpec(
            num_scalar_prefetch=3, grid=(ntiles,),
            in_specs=[tok_spec, tok_spec], out_specs=[out_spec, out_spec]),
        out_shape=[jax.ShapeDtypeStruct((e, cap // LANES, LANES), F32)] * 2,
        compiler_params=_params(("arbitrary",)),
        name="slot_lists",
    )(base.reshape(-1), end.reshape(-1), extra, pos, aff)
    return idx.astype(jnp.int32).reshape(e * cap), gates.reshape(e, cap)


WIN_ROWS = 64
WIN_ALIGN = 16
WIN_BUFFERS = 3


def _combine_kernel(lo_ref, end_ref, extra_ref, pos_ref, x1_ref, mod_ref, ye_hbm, o_ref,
                    ystack, wt_ref, xbuf, sems, xsem, *, cap, ntiles):
    i = pl.program_id(0)
    cur = i % WIN_BUFFERS

    def window_lo(tile, e):
        return lo_ref[e * ntiles + tile]

    def window_copy(tile, e, buf):
        st = pl.multiple_of(jnp.minimum(window_lo(tile, e), cap - WIN_ROWS), WIN_ALIGN)
        return pltpu.make_async_copy(ye_hbm.at[e, pl.ds(st, WIN_ROWS), :],
                                     ystack.at[buf, pl.ds(e * WIN_ROWS, WIN_ROWS), :], sems.at[buf])

    @pl.when(i == 0)
    def _():
        for t in range(min(WIN_BUFFERS - 1, ntiles)):
            for e in range(N_EXPERTS):
                window_copy(t, e, t).start()

    @pl.when(i + WIN_BUFFERS - 1 < ntiles)
    def _():
        for e in range(N_EXPERTS):
            window_copy(i + WIN_BUFFERS - 1, e, (i + WIN_BUFFERS - 1) % WIN_BUFFERS).start()

    slot = lax.broadcasted_iota(jnp.int32, (WIN_ROWS, TOK_TILE), 0)

    def one_hot(p, lo):
        st = jnp.minimum(lo, cap - WIN_ROWS)
        rel = jnp.where((p >= lo) & (p < lo + WIN_ROWS), p - st, -1)
        return jnp.where(rel == slot, 1.0, 0.0).astype(BF16)

    for e in range(N_EXPERTS):
        wt_ref[e * WIN_ROWS:(e + 1) * WIN_ROWS, :] = one_hot(pos_ref[e:e + 1, :], window_lo(i, e))
    if cap >= N_EXPERTS * WIN_ROWS:
        pltpu.make_async_copy(ye_hbm.at[0, pl.ds(0, N_EXPERTS * WIN_ROWS), :], ystack.at[cur], sems.at[cur]).wait()
    else:
        for e in range(N_EXPERTS):
            window_copy(i, e, cur).wait()
    gate2 = mod_ref[5:6, :]
    o_ref[...] = x1_ref[...] + gate2 * lax.dot_general(
        wt_ref[...], ystack[cur], (((0,), (0,)), ((), ())), preferred_element_type=F32)

    @pl.when(extra_ref[i] > 0)
    def _():
        def per_expert(e, carry):
            lo = window_lo(i, e)
            nwin = (end_ref[e * ntiles + i] - lo + WIN_ROWS - 1) // WIN_ROWS
            p = pos_ref[pl.ds(e, 1), :]

            def per_window(w, c):
                lo_w = lo + w * WIN_ROWS
                st = pl.multiple_of(jnp.minimum(lo_w, cap - WIN_ROWS), WIN_ALIGN)
                cp = pltpu.make_async_copy(ye_hbm.at[e, pl.ds(st, WIN_ROWS), :], xbuf, xsem)
                cp.start()
                cp.wait()
                o_ref[...] += gate2 * lax.dot_general(one_hot(p, lo_w), xbuf[...], (((0,), (0,)), ((), ())),
                                                      preferred_element_type=F32)
                return c

            lax.fori_loop(1, nwin, per_window, 0)
            return carry

        lax.fori_loop(0, N_EXPERTS, per_expert, 0)


def _combine(base, end, pos, x1, mod, ye):
    e, n = pos.shape
    cap, d = ye.shape[1], ye.shape[2]
    ntiles = n // TOK_TILE
    lo = (base // WIN_ALIGN) * WIN_ALIGN
    extra = jnp.max(jnp.maximum((end - lo + WIN_ROWS - 1) // WIN_ROWS - 1, 0), axis=0).astype(jnp.int32)
    tiles_per_seq = SEQ // TOK_TILE
    return pl.pallas_call(
        functools.partial(_combine_kernel, cap=cap, ntiles=ntiles),
        grid_spec=pltpu.PrefetchScalarGridSpec(
            num_scalar_prefetch=3, grid=(ntiles,),
            in_specs=[
                pl.BlockSpec((e, TOK_TILE), lambda i, *_: (0, i)),
                pl.BlockSpec((TOK_TILE, d), lambda i, *_: (i, 0)),
                pl.BlockSpec((None, N_MOD, d), lambda i, *_: (i // tiles_per_seq, 0, 0)),
                pl.BlockSpec(memory_space=pl.ANY),
            ],
            out_specs=pl.BlockSpec((TOK_TILE, d), lambda i, *_: (i, 0)),
            scratch_shapes=[
                pltpu.VMEM((WIN_BUFFERS, e * WIN_ROWS, d), BF16),
                pltpu.VMEM((e * WIN_ROWS, TOK_TILE), BF16),
                pltpu.VMEM((WIN_ROWS, d), BF16),
                pltpu.SemaphoreType.DMA((WIN_BUFFERS,)),
                pltpu.SemaphoreType.DMA(()),
            ]),
        out_shape=jax.ShapeDtypeStruct((n, d), F32),
        compiler_params=_params(("arbitrary",)),
        name="combine",
    )(lo.reshape(-1), end.reshape(-1), extra, pos, x1, mod, ye)


FFN_ROWS = 2048
FFN_SUB = 512
FFN_FT = 512


def _ffn_kernel(idx_ref, idx_next_ref, h_hbm, wg_ref, wu_ref, wd_ref, gate_ref, o_ref,
                xbuf, xb_ref, acc_ref, sem, *, nblk, nf):
    f = pl.program_id(2)
    blk = pl.program_id(0) * pl.num_programs(1) + pl.program_id(1)
    rows_blk = xb_ref.shape[0]
    per_step = rows_blk // nf

    def row_copy(idx, k, j):
        return pltpu.make_async_copy(h_hbm.at[pl.ds(idx[k * per_step + j], 1), :],
                                     xbuf.at[k, pl.ds(j, 1), :], sem)

    def wait_block():
        for k in range(nf):
            pltpu.make_async_copy(h_hbm.at[pl.ds(0, per_step), :], xbuf.at[k], sem).wait()

    @pl.when((blk == 0) & (f == 0))
    def _():
        for k in range(nf):
            def issue(j, carry):
                row_copy(idx_ref, k, j).start()
                return carry
            lax.fori_loop(0, per_step, issue, 0)

    @pl.when(f == 0)
    def _():
        wait_block()
        for k in range(nf):
            xb_ref[k * per_step:(k + 1) * per_step, :] = xbuf[k].astype(BF16)
        acc_ref[...] = jnp.zeros_like(acc_ref)

    for j in range(per_step):
        row_copy(idx_next_ref, f, j).start()

    wg = wg_ref[...].astype(BF16)
    wu = wu_ref[...].astype(BF16)
    wd = wd_ref[...].astype(BF16)
    sub = min(FFN_SUB, rows_blk)
    for t in range(rows_blk // sub):
        rows = slice(t * sub, (t + 1) * sub)
        xt = xb_ref[rows, :]
        a = _dot(xt, wg)
        b = _dot(xt, wu)
        acc_ref[rows, :] += _dot((_silu(a) * b).astype(BF16), wd)

    @pl.when(f == nf - 1)
    def _():
        o_ref[...] = (acc_ref[...] * gate_ref[...]).astype(o_ref.dtype)

    @pl.when((blk == nblk - 1) & (f == nf - 1))
    def _():
        wait_block()


def _expert_ffn(idx, h, w_gate, w_up, w_down, gates, layer):
    e, cap = gates.shape[0], gates.shape[1]
    d = h.shape[1]
    ff = w_gate.shape[3]
    rb = min(cap, FFN_ROWS)
    nrb = cap // rb
    nblk = e * nrb
    nf = ff // FFN_FT
    return pl.pallas_call(
        functools.partial(_ffn_kernel, nblk=nblk, nf=nf),
        grid=(e, nrb, nf),
        in_specs=[
            pl.BlockSpec((rb,), lambda i, r, f: (i * nrb + r,), memory_space=pltpu.SMEM),
            pl.BlockSpec((rb,), lambda i, r, f: (jnp.minimum(i * nrb + r + 1, nblk - 1),),
                         memory_space=pltpu.SMEM),
            pl.BlockSpec(memory_space=pl.ANY),
            pl.BlockSpec((None, None, d, FFN_FT), lambda i, r, f: (layer, i, 0, f)),
            pl.BlockSpec((None, None, d, FFN_FT), lambda i, r, f: (layer, i, 0, f)),
            pl.BlockSpec((None, None, FFN_FT, d), lambda i, r, f: (layer, i, f, 0)),
            pl.BlockSpec((None, rb, 1), lambda i, r, f: (i, r, 0)),
        ],
        out_specs=pl.BlockSpec((None, rb, d), lambda i, r, f: (i, r, 0)),
        out_shape=jax.ShapeDtypeStruct((e, cap, d), BF16),
        scratch_shapes=[
            pltpu.VMEM((nf, rb // nf, d), F32),
            pltpu.VMEM((rb, d), BF16),
            pltpu.VMEM((rb, d), F32),
            pltpu.SemaphoreType.DMA(()),
        ],
        compiler_params=_params(("arbitrary", "arbitrary", "arbitrary")),
        name="expert_ffn",
    )(idx, idx, h, w_gate, w_up, w_down, gates)


def _moe(h2, logits, x1, mod, wg, wu, wd, layer):
    n, d = h2.shape
    cap = max(1, EC_FACTOR * n // N_EXPERTS)
    lt = logits.reshape(N_EXPERTS, n // LANES, LANES)
    aff, pos, rank = _route(lt, cap)
    aff = aff.reshape(N_EXPERTS, n)
    pos = pos.reshape(N_EXPERTS, n)
    base = rank.reshape(N_EXPERTS, n)[:, ::TOK_TILE]
    end = jnp.concatenate([base[:, 1:], jnp.full((N_EXPERTS, 1), cap, jnp.int32)], axis=1)
    idx, gates = _invert(base, end, pos, aff, cap)
    ye = _expert_ffn(idx, h2, wg, wu, wd, gates[..., None], layer)
    return _combine(base, end, pos, x1, mod, ye)


def _trunk(x, c, p):
    b = x.shape[0]
    n = b * SEQ
    mod = _ada(c, p["ada_w"], p["ada_b"])
    xf = x.reshape(n, D_MODEL)
    for i in range(2):
        g1 = p["norm1_g"][i].reshape(1, D_MODEL)
        g2 = p["norm2_g"][i].reshape(1, D_MODEL)
        if i == 0:
            qkv = _norm_qkv(xf, mod[i], g1, p["w_qkv"])
            m = _attention(qkv.reshape(b, SEQ, 3 * D_MODEL), p["q_gain"], p["k_gain"], p["bias"])
            w_mix = p["w_o"]
        else:
            v = _fnet_in(xf, mod[i], g1, p["w_in"], p["cs"])
            m = _seq_dft(v.reshape(b, SEQ, 2 * D_MODEL), p["ct"], p["st_neg"])
            w_mix = p["w_out"]
        x1, h2, logits = _mix_out(m.reshape(n, D_MODEL), w_mix, xf, mod[i], g2, p["w_router"][i])
        xf = _moe(h2, logits, x1, mod[i], p["w_gate"], p["w_up"], p["w_down"], i)
    return xf.reshape(b, SEQ, D_MODEL)


def _split_router(w):
    wt = jnp.swapaxes(w, 1, 2)
    hi = wt.astype(BF16)
    lo = (wt - hi.astype(F32)).astype(BF16)
    return jnp.concatenate([hi, lo], axis=1)


def kernel(x_prompt, x_sample, c_prompt, c_sample, norm1_g, norm2_g, ada_w, ada_b, na_w_qkv, na_q_g, na_k_g, na_rpb, na_w_o, fn_w_in, fn_w_out, moe_w_router, moe_w_gate, moe_w_up, moe_w_down):
    cs, ct, st_neg = _dft_tables()
    tile2 = lambda g: jnp.tile(g.reshape(1, HEAD_DIM), (1, HEADS_PER_BLOCK))
    p = {
        "norm1_g": norm1_g, "norm2_g": norm2_g, "ada_w": ada_w, "ada_b": ada_b,
        "w_qkv": na_w_qkv[0].astype(BF16),
        "q_gain": tile2(na_q_g[0]) * (HEAD_DIM ** -0.5),
        "k_gain": tile2(na_k_g[0]),
        "bias": _attn_bias_tables(na_rpb[0]),
        "w_o": na_w_o[0].astype(BF16),
        "w_in": fn_w_in[0].astype(BF16),
        "w_out": fn_w_out[0].astype(BF16),
        "cs": cs, "ct": ct, "st_neg": st_neg,
        "w_router": _split_router(moe_w_router),
        "w_gate": moe_w_gate, "w_up": moe_w_up, "w_down": moe_w_down,
    }
    return (_trunk(x_prompt, c_prompt, p), _trunk(x_sample, c_sample, p))
```

```python
import functools

import numpy as np
import jax
import jax.numpy as jnp
from jax import lax
from jax.experimental import pallas as pl
from jax.experimental.pallas import tpu as pltpu

D_MODEL = 1024
SEQ = 2048
GRID_W = 64
ROWS = SEQ // GRID_W
NA_HEADS = 16
HEAD_DIM = D_MODEL // NA_HEADS
WIN_R = 8
WIN_C = 16
FN_GROUPS = 4
FN_GROUP_DIM = D_MODEL // FN_GROUPS
N_EXPERTS = 16
EC_FACTOR = 2
EXPERT_FF = 2048
N_MOD = 6
EPS = 1e-6
NEG = -1e30

LANES = 128
KEY_ROWS = min(WIN_R, ROWS)
KEYS = KEY_ROWS * GRID_W
HEADS_PER_BLOCK = LANES // HEAD_DIM
N_ROW_OFFSETS = KEY_ROWS

VMEM_LIMIT = 52 * 1024 * 1024

F32 = jnp.float32
BF16 = jnp.bfloat16


def _params(sem):
    return pltpu.CompilerParams(dimension_semantics=sem, vmem_limit_bytes=VMEM_LIMIT)


def _dot(a, b):
    return jnp.dot(a, b, preferred_element_type=F32)


def _dot_nt(a, b):
    return lax.dot_general(a, b, (((1,), (1,)), ((), ())), preferred_element_type=F32)


def _dot_split(a, b):
    a_hi = a.astype(BF16)
    a_lo = (a - a_hi.astype(F32)).astype(BF16)
    b_hi = b.astype(BF16)
    b_lo = (b - b_hi.astype(F32)).astype(BF16)
    return _dot(a_hi, b_hi) + (_dot(a_hi, b_lo) + _dot(a_lo, b_hi))


def _silu(a):
    return a * (1.0 / (1.0 + jnp.exp(-a)))


def _rms_mod(x, gain, shift, scale):
    ms = jnp.mean(x * x, axis=-1, keepdims=True)
    y = x * lax.rsqrt(ms + EPS) * gain
    return y * (1.0 + scale) + shift


def _ada_kernel(c_ref, w_ref, b_ref, o_ref):
    o_ref[...] = _dot_split(_silu(c_ref[...]), w_ref[...]) + b_ref[...]


def _ada(c, ada_w, ada_b):
    depth, d, n6 = ada_w.shape
    b = c.shape[0]
    tn = 1536
    out = pl.pallas_call(
        _ada_kernel,
        grid=(depth, n6 // tn),
        in_specs=[
            pl.BlockSpec((b, d), lambda i, j: (0, 0)),
            pl.BlockSpec((None, d, tn), lambda i, j: (i, 0, j)),
            pl.BlockSpec((None, 1, tn), lambda i, j: (i, 0, j)),
        ],
        out_specs=pl.BlockSpec((None, b, tn), lambda i, j: (i, 0, j)),
        out_shape=jax.ShapeDtypeStruct((depth, b, n6), F32),
        compiler_params=_params(("arbitrary", "arbitrary")),
        name="ada_mod",
    )(c, ada_w, ada_b.reshape(depth, 1, n6))
    return out.reshape(depth, b, N_MOD, d)


ROW_TILE = 1024


def _qkv_kernel(x_ref, mod_ref, g_ref, w_ref, o_ref):
    h = _rms_mod(x_ref[...], g_ref[...], mod_ref[0:1, :], mod_ref[1:2, :])
    o_ref[...] = _dot(h.astype(BF16), w_ref[...]).astype(BF16)


def _norm_qkv(x, mod, gain, w):
    n, d = x.shape
    nn = w.shape[1]
    tiles_per_seq = SEQ // ROW_TILE
    return pl.pallas_call(
        _qkv_kernel,
        grid=(n // ROW_TILE,),
        in_specs=[
            pl.BlockSpec((ROW_TILE, d), lambda i: (i, 0)),
            pl.BlockSpec((None, N_MOD, d), lambda i: (i // tiles_per_seq, 0, 0)),
            pl.BlockSpec((1, d), lambda i: (0, 0)),
            pl.BlockSpec((d, nn), lambda i: (0, 0)),
        ],
        out_specs=pl.BlockSpec((ROW_TILE, nn), lambda i: (i, 0)),
        out_shape=jax.ShapeDtypeStruct((n, nn), BF16),
        compiler_params=_params(("arbitrary",)),
        name="norm_qkv",
    )(x, mod, gain, w)


def _fnet_in_kernel(x_ref, mod_ref, g_ref, w_ref, cs_ref, o_ref):
    h = _rms_mod(x_ref[...], g_ref[...], mod_ref[0:1, :], mod_ref[1:2, :])
    u = _dot(h.astype(BF16), w_ref[...]).astype(BF16)
    gd = FN_GROUP_DIM
    for g in range(FN_GROUPS):
        v = _dot(u[:, g * gd:(g + 1) * gd], cs_ref[...])
        o_ref[:, g * gd:(g + 1) * gd] = v[:, :gd].astype(BF16)
        o_ref[:, D_MODEL + g * gd:D_MODEL + (g + 1) * gd] = v[:, gd:].astype(BF16)


def _fnet_in(x, mod, gain, w, cs):
    n, d = x.shape
    tiles_per_seq = SEQ // ROW_TILE
    return pl.pallas_call(
        _fnet_in_kernel,
        grid=(n // ROW_TILE,),
        in_specs=[
            pl.BlockSpec((ROW_TILE, d), lambda i: (i, 0)),
            pl.BlockSpec((None, N_MOD, d), lambda i: (i // tiles_per_seq, 0, 0)),
            pl.BlockSpec((1, d), lambda i: (0, 0)),
            pl.BlockSpec((d, d), lambda i: (0, 0)),
            pl.BlockSpec(cs.shape, lambda i: (0, 0)),
        ],
        out_specs=pl.BlockSpec((ROW_TILE, 2 * d), lambda i: (i, 0)),
        out_shape=jax.ShapeDtypeStruct((n, 2 * d), BF16),
        compiler_params=_params(("arbitrary",)),
        name="fnet_in",
    )(x, mod, gain, w, cs)


SEQ_DFT_TILE = 1024
SEQ_DFT_SUB = 256


def _seq_dft_kernel(ct_ref, st_ref, v_ref, o_ref):
    scale = 1.0 / np.sqrt(SEQ)
    for s in range(SEQ_DFT_TILE // SEQ_DFT_SUB):
        rows = slice(s * SEQ_DFT_SUB, (s + 1) * SEQ_DFT_SUB)
        acc = _dot(ct_ref[rows, :], v_ref[:, :D_MODEL]) + _dot(st_ref[rows, :], v_ref[:, D_MODEL:])
        o_ref[rows, :] = (acc * scale).astype(BF16)


def _seq_dft(v, ct, st_neg):
    b = v.shape[0]
    return pl.pallas_call(
        _seq_dft_kernel,
        grid=(SEQ // SEQ_DFT_TILE, b),
        in_specs=[
            pl.BlockSpec((SEQ_DFT_TILE, SEQ), lambda k, i: (k, 0)),
            pl.BlockSpec((SEQ_DFT_TILE, SEQ), lambda k, i: (k, 0)),
            pl.BlockSpec((None, SEQ, 2 * D_MODEL), lambda k, i: (i, 0, 0)),
        ],
        out_specs=pl.BlockSpec((None, SEQ_DFT_TILE, D_MODEL), lambda k, i: (i, k, 0)),
        out_shape=jax.ShapeDtypeStruct((b, SEQ, D_MODEL), BF16),
        compiler_params=_params(("arbitrary", "arbitrary")),
        name="seq_dft",
    )(ct, st_neg, v)


def _dft_tables():
    def cos_sin(n):
        k = lax.broadcasted_iota(jnp.int32, (n, n), 0)
        t = lax.broadcasted_iota(jnp.int32, (n, n), 1)
        ang = ((k * t) % n).astype(F32) * (2.0 * np.pi / n)
        return jnp.cos(ang), jnp.sin(ang)
    cc, sc = cos_sin(FN_GROUP_DIM)
    cs = (jnp.concatenate([cc, sc], axis=1) * (1.0 / np.sqrt(FN_GROUP_DIM))).astype(BF16)
    ct, st = cos_sin(SEQ)
    return cs, ct.astype(BF16), (-st).astype(BF16)


def _bias_cols_kernel(rpb_ref, select_ref, valid_ref, o_ref):
    r = rpb_ref[...]
    hi = r.astype(BF16)
    r1 = r - hi.astype(F32)
    mid = r1.astype(BF16)
    low = (r1 - mid.astype(F32)).astype(BF16)
    sel = select_ref[...]
    t = _dot(hi, sel) + (_dot(mid, sel) + _dot(low, sel))
    o_ref[...] = jnp.where(valid_ref[...] > 0.0, t, NEG)


def _attn_bias_tables(rpb):
    h, nr, nc = rpb.shape
    q = np.arange(GRID_W)
    kc = np.arange(GRID_W)
    start = np.clip(q - WIN_C // 2, 0, GRID_W - WIN_C)
    rel = kc[None, :] - start[:, None]
    mask = (rel >= 0) & (rel < WIN_C)
    dcol = np.clip(kc[None, :] - q[:, None] + WIN_C - 1, 0, 2 * WIN_C - 2)
    pairs = GRID_W * GRID_W
    select = np.zeros((LANES, pairs), np.float32)
    select[dcol.reshape(-1), np.arange(pairs)] = 1.0
    cols = pl.pallas_call(
        _bias_cols_kernel,
        out_shape=jax.ShapeDtypeStruct((h * nr, pairs), F32),
        name="bias_cols",
    )(jnp.pad(rpb.reshape(h * nr, nc).astype(F32), ((0, 0), (0, LANES - nc))),
      jnp.asarray(select, BF16), jnp.asarray(mask.reshape(1, pairs), F32))
    cols = cols.reshape(h, nr, GRID_W, GRID_W)
    tbl = jnp.stack([cols[:, WIN_R - 1 - o:WIN_R - 1 - o + KEY_ROWS] for o in range(N_ROW_OFFSETS)], axis=1)
    tbl = tbl.transpose(0, 1, 3, 2, 4)
    return tbl.reshape(NA_HEADS // HEADS_PER_BLOCK, HEADS_PER_BLOCK, N_ROW_OFFSETS, GRID_W, KEYS)


ATTN_GROUP = 8


def _attn_kernel(q_ref, k_ref, v_ref, qg_ref, kg_ref, bias_ref, o_ref, qn_ref, kn_ref, s_ref, p_ref):
    lane = lax.broadcasted_iota(jnp.int32, (1, LANES), 1)
    lo = lane < HEAD_DIM

    ia = lax.broadcasted_iota(jnp.int32, (LANES, LANES), 0) // HEAD_DIM
    ib = lax.broadcasted_iota(jnp.int32, (LANES, LANES), 1) // HEAD_DIM
    head_mean = jnp.where(ia == ib, 1.0 / HEAD_DIM, 0.0).astype(BF16)

    def head_norm(x, g):
        ms = _dot((x * x).astype(BF16), head_mean)
        return x * lax.rsqrt(ms + EPS) * g

    qn_ref[...] = head_norm(q_ref[...].astype(F32), qg_ref[...]).astype(BF16)
    kn_ref[...] = head_norm(k_ref[...].astype(F32), kg_ref[...]).astype(BF16)

    def group(g, carry):
        starts = []
        for j in range(ATTN_GROUP):
            r = g * ATTN_GROUP + j
            rs = min(max(r - KEY_ROWS // 2, 0), ROWS - KEY_ROWS)
            off = r - rs
            k0 = rs * GRID_W
            q0 = r * GRID_W
            starts.append((k0, q0))
            kr = kn_ref[pl.ds(k0, KEYS), :]
            qr = qn_ref[pl.ds(q0, GRID_W), :]
            for h in range(HEADS_PER_BLOCK):
                keep = lo if h == 0 else jnp.logical_not(lo)
                qm = jnp.where(keep, qr, jnp.zeros_like(qr))
                s_ref[j * HEADS_PER_BLOCK + h] = _dot_nt(qm, kr) + bias_ref[h, off]
        inv = []
        for u in range(ATTN_GROUP * HEADS_PER_BLOCK):
            s = s_ref[u]
            p = jnp.exp(s - jnp.max(s, axis=-1, keepdims=True))
            inv.append(1.0 / jnp.sum(p, axis=-1, keepdims=True))
            p_ref[u] = p.astype(BF16)
        for j in range(ATTN_GROUP):
            k0, q0 = starts[j]
            vr = v_ref[pl.ds(k0, KEYS), :]
            outs = [_dot(p_ref[j * HEADS_PER_BLOCK + h], vr) * inv[j * HEADS_PER_BLOCK + h]
                    for h in range(HEADS_PER_BLOCK)]
            o_ref[pl.ds(q0, GRID_W), :] = jnp.where(lo, outs[0], outs[1]).astype(BF16)
        return carry

    for g in range(ROWS // ATTN_GROUP):
        group(g, 0)


def _attention(qkv, q_gain, k_gain, bias):
    b = qkv.shape[0]
    nblk = D_MODEL // LANES
    blk = lambda base: pl.BlockSpec((None, SEQ, LANES), lambda hp, i: (i, 0, base + hp))
    return pl.pallas_call(
        _attn_kernel,
        grid=(nblk, b),
        in_specs=[
            blk(0), blk(nblk), blk(2 * nblk),
            pl.BlockSpec((1, LANES), lambda hp, i: (0, 0)),
            pl.BlockSpec((1, LANES), lambda hp, i: (0, 0)),
            pl.BlockSpec((None, HEADS_PER_BLOCK, N_ROW_OFFSETS, GRID_W, KEYS),
                         lambda hp, i: (hp, 0, 0, 0, 0)),
        ],
        out_specs=pl.BlockSpec((None, SEQ, LANES), lambda hp, i: (i, 0, hp)),
        out_shape=jax.ShapeDtypeStruct((b, SEQ, D_MODEL), BF16),
        scratch_shapes=[
            pltpu.VMEM((SEQ, LANES), BF16), pltpu.VMEM((SEQ, LANES), BF16),
            pltpu.VMEM((ATTN_GROUP * HEADS_PER_BLOCK, GRID_W, KEYS), F32),
            pltpu.VMEM((ATTN_GROUP * HEADS_PER_BLOCK, GRID_W, KEYS), BF16),
        ],
        compiler_params=_params(("arbitrary", "arbitrary")),
        name="nbr_attention",
    )(qkv, qkv, qkv, q_gain, k_gain, bias)


def _mix_out_kernel(m_ref, w_ref, x_ref, mod_ref, g_ref, wr_ref, x1_ref, h2_ref, lg_ref):
    x1 = x_ref[...] + mod_ref[2:3, :] * _dot(m_ref[...], w_ref[...])
    x1_ref[...] = x1
    h2 = _rms_mod(x1, g_ref[...], mod_ref[3:4, :], mod_ref[4:5, :])
    h2_ref[...] = h2
    h_hi = h2.astype(BF16)
    h_lo = (h2 - h_hi.astype(F32)).astype(BF16)
    a = _dot_nt(wr_ref[...], h_hi)
    b = _dot_nt(wr_ref[0:N_EXPERTS, :], h_lo)
    lg_ref[...] = a[0:N_EXPERTS] + (a[N_EXPERTS:] + b)


def _mix_out(m, w, x, mod, gain, w_router):
    n, d = x.shape
    tiles_per_seq = SEQ // ROW_TILE
    row = lambda i: (i, 0)
    fixed = lambda i: (0, 0)
    return pl.pallas_call(
        _mix_out_kernel,
        grid=(n // ROW_TILE,),
        in_specs=[
            pl.BlockSpec((ROW_TILE, d), row),
            pl.BlockSpec((d, d), fixed),
            pl.BlockSpec((ROW_TILE, d), row),
            pl.BlockSpec((None, N_MOD, d), lambda i: (i // tiles_per_seq, 0, 0)),
            pl.BlockSpec((1, d), fixed),
            pl.BlockSpec((2 * N_EXPERTS, d), fixed),
        ],
        out_specs=[
            pl.BlockSpec((ROW_TILE, d), row),
            pl.BlockSpec((ROW_TILE, d), row),
            pl.BlockSpec((N_EXPERTS, ROW_TILE), lambda i: (0, i)),
        ],
        out_shape=[
            jax.ShapeDtypeStruct((n, d), F32),
            jax.ShapeDtypeStruct((n, d), F32),
            jax.ShapeDtypeStruct((N_EXPERTS, n), F32),
        ],
        compiler_params=_params(("arbitrary",)),
        name="mix_out",
    )(m, w, x, mod, gain, w_router)


def _route_kernel(lg_ref, aff_ref, pos_ref, rank_ref, *, cap):
    lg = lg_ref[...]
    rows = lg.shape[1]
    mx = jnp.max(lg, axis=0, keepdims=True)
    ex = jnp.exp(lg - mx)
    aff = ex / jnp.sum(ex, axis=0, keepdims=True)
    aff_ref[...] = aff

    def count(mask):
        c = jnp.sum(mask.astype(jnp.int32), axis=2, keepdims=True)
        return jnp.sum(c, axis=1, keepdims=True)

    def as_float(bits):
        return lax.bitcast_convert_type(bits, F32)

    def search(i, thr):
        cand = thr | jnp.left_shift(jnp.int32(1), 30 - i)
        return jnp.where(count(aff >= as_float(cand)) >= cap, cand, thr)

    thr = as_float(lax.fori_loop(0, 31, search, jnp.zeros((N_EXPERTS, 1, 1), jnp.int32)))
    gt = aff > thr
    eq = aff == thr
    need = cap - count(gt)

    a = lax.broadcasted_iota(jnp.int32, (LANES, LANES), 0)
    b = lax.broadcasted_iota(jnp.int32, (LANES, LANES), 1)
    upper = (a <= b).astype(BF16)
    ones = jnp.ones((LANES, LANES), BF16)
    ra = lax.broadcasted_iota(jnp.int32, (rows, rows), 0)
    rb = lax.broadcasted_iota(jnp.int32, (rows, rows), 1)
    lower = (rb < ra).astype(BF16)

    def excl_cumsum(mask):
        mb = mask.astype(BF16)
        within = _dot(mb, upper)
        before = _dot(lower, _dot(mb, ones).astype(BF16))
        return (within + before).astype(jnp.int32) - mask.astype(jnp.int32)

    for e in range(N_EXPERTS):
        sel = gt[e] | (eq[e] & (excl_cumsum(eq[e]) < need[e]))
        rank = excl_cumsum(sel)
        rank_ref[e] = rank
        pos_ref[e] = jnp.where(sel, rank, -1)


def _route(logits_t, cap):
    shape = logits_t.shape
    spec = pl.BlockSpec(shape, lambda: (0, 0, 0))
    return pl.pallas_call(
        functools.partial(_route_kernel, cap=cap),
        in_specs=[spec],
        out_specs=[spec, spec, spec],
        out_shape=[jax.ShapeDtypeStruct(shape, F32), jax.ShapeDtypeStruct(shape, jnp.int32),
                   jax.ShapeDtypeStruct(shape, jnp.int32)],
        compiler_params=pltpu.CompilerParams(vmem_limit_bytes=VMEM_LIMIT),
        name="route",
    )(logits_t)


TOK_TILE = 256


def _invert_kernel(base_ref, end_ref, extra_ref, pos_ref, aff_ref, idx_ref, gate_ref, *, ntiles):
    i = pl.program_id(0)
    nchunks = idx_ref.shape[1]

    @pl.when(i == 0)
    def _():
        idx_ref[...] = jnp.zeros_like(idx_ref)
        gate_ref[...] = jnp.zeros_like(gate_ref)

    local = lax.broadcasted_iota(jnp.int32, (1, TOK_TILE), 1).astype(F32)
    slot = lax.broadcasted_iota(jnp.int32, (LANES, TOK_TILE), 0)
    tile_start = (i * TOK_TILE).astype(F32)
    zero = jnp.zeros((11, TOK_TILE), BF16)

    def lhs_rows(g):
        g_hi = g.astype(BF16)
        r1 = g - g_hi.astype(F32)
        g_mid = r1.astype(BF16)
        g_lo = (r1 - g_mid.astype(F32)).astype(BF16)
        return jnp.concatenate([local.astype(BF16), jnp.ones((1, TOK_TILE), BF16), g_hi, g_mid, g_lo, zero], axis=0)

    def add_chunk(e, c, p, lhs):
        hit = jnp.where(p - c * LANES == slot, 1.0, 0.0).astype(BF16)
        r = _dot_nt(lhs, hit)
        row = pl.ds(jnp.minimum(c, nchunks - 1), 1)
        idx_ref[e, row, :] += r[0:1] + tile_start * r[1:2]
        gate_ref[e, row, :] += r[2:3] + r[3:4] + r[4:5]

    for e in range(N_EXPERTS):
        p = pos_ref[e:e + 1, :]
        lhs = lhs_rows(aff_ref[e:e + 1, :])
        first = base_ref[e * ntiles + i] // LANES
        add_chunk(e, first, p, lhs)
        add_chunk(e, first + 1, p, lhs)

    @pl.when(extra_ref[i] > 0)
    def _():
        def per_expert(e, carry):
            p = pos_ref[pl.ds(e, 1), :]
            lhs = lhs_rows(aff_ref[pl.ds(e, 1), :])
            first = base_ref[e * ntiles + i] // LANES
            last = (end_ref[e * ntiles + i] + LANES - 1) // LANES

            def per_chunk(c, c2):
                add_chunk(e, c, p, lhs)
                return c2

            lax.fori_loop(first + 2, last, per_chunk, 0)
            return carry

        lax.fori_loop(0, N_EXPERTS, per_expert, 0)


def _invert(base, end, pos, aff, cap):
    e, n = pos.shape
    ntiles = n // TOK_TILE
    nchunks_tile = (end + LANES - 1) // LANES - base // LANES
    extra = jnp.max(jnp.maximum(nchunks_tile - 2, 0), axis=0).astype(jnp.int32)
    out_spec = pl.BlockSpec((e, cap // LANES, LANES), lambda i, *_: (0, 0, 0))
    tok_spec = pl.BlockSpec((e, TOK_TILE), lambda i, *_: (0, i))
    idx, gates = pl.pallas_call(
        functools.partial(_invert_kernel, ntiles=ntiles),
        grid_spec=pltpu.PrefetchScalarGridSpec(
            num_scalar_prefetch=3, grid=(ntiles,),
            in_specs=[tok_spec, tok_spec], out_specs=[out_spec, out_spec]),
        out_shape=[jax.ShapeDtypeStruct((e, cap // LANES, LANES), F32)] * 2,
        compiler_params=_params(("arbitrary",)),
        name="slot_lists",
    )(base.reshape(-1), end.reshape(-1), extra, pos, aff)
    return idx.astype(jnp.int32).reshape(e * cap), gates.reshape(e, cap)


WIN_ROWS = 64
WIN_ALIGN = 16
WIN_BUFFERS = 3


def _combine_kernel(lo_ref, end_ref, extra_ref, pos_ref, x1_ref, mod_ref, ye_hbm, o_ref,
                    ystack, wt_ref, xbuf, sems, xsem, *, cap, ntiles):
    i = pl.program_id(0)
    cur = i % WIN_BUFFERS

    def window_lo(tile, e):
        return lo_ref[e * ntiles + tile]

    def window_copy(tile, e, buf):
        st = pl.multiple_of(jnp.minimum(window_lo(tile, e), cap - WIN_ROWS), WIN_ALIGN)
        return pltpu.make_async_copy(ye_hbm.at[e, pl.ds(st, WIN_ROWS), :],
                                     ystack.at[buf, pl.ds(e * WIN_ROWS, WIN_ROWS), :], sems.at[buf])

    @pl.when(i == 0)
    def _():
        for t in range(min(WIN_BUFFERS - 1, ntiles)):
            for e in range(N_EXPERTS):
                window_copy(t, e, t).start()

    @pl.when(i + WIN_BUFFERS - 1 < ntiles)
    def _():
        for e in range(N_EXPERTS):
            window_copy(i + WIN_BUFFERS - 1, e, (i + WIN_BUFFERS - 1) % WIN_BUFFERS).start()

    slot = lax.broadcasted_iota(jnp.int32, (WIN_ROWS, TOK_TILE), 0)

    def one_hot(p, lo):
        st = jnp.minimum(lo, cap - WIN_ROWS)
        rel = jnp.where((p >= lo) & (p < lo + WIN_ROWS), p - st, -1)
        return jnp.where(rel == slot, 1.0, 0.0).astype(BF16)

    for e in range(N_EXPERTS):
        wt_ref[e * WIN_ROWS:(e + 1) * WIN_ROWS, :] = one_hot(pos_ref[e:e + 1, :], window_lo(i, e))
    if cap >= N_EXPERTS * WIN_ROWS:
        pltpu.make_async_copy(ye_hbm.at[0, pl.ds(0, N_EXPERTS * WIN_ROWS), :], ystack.at[cur], sems.at[cur]).wait()
    else:
        for e in range(N_EXPERTS):
            window_copy(i, e, cur).wait()
    gate2 = mod_ref[5:6, :]
    o_ref[...] = x1_ref[...] + gate2 * lax.dot_general(
        wt_ref[...], ystack[cur], (((0,), (0,)), ((), ())), preferred_element_type=F32)

    @pl.when(extra_ref[i] > 0)
    def _():
        def per_expert(e, carry):
            lo = window_lo(i, e)
            nwin = (end_ref[e * ntiles + i] - lo + WIN_ROWS - 1) // WIN_ROWS
            p = pos_ref[pl.ds(e, 1), :]

            def per_window(w, c):
                lo_w = lo + w * WIN_ROWS
                st = pl.multiple_of(jnp.minimum(lo_w, cap - WIN_ROWS), WIN_ALIGN)
                cp = pltpu.make_async_copy(ye_hbm.at[e, pl.ds(st, WIN_ROWS), :], xbuf, xsem)
                cp.start()
                cp.wait()
                o_ref[...] += gate2 * lax.dot_general(one_hot(p, lo_w), xbuf[...], (((0,), (0,)), ((), ())),
                                                      preferred_element_type=F32)
                return c

            lax.fori_loop(1, nwin, per_window, 0)
            return carry

        lax.fori_loop(0, N_EXPERTS, per_expert, 0)


def _combine(base, end, pos, x1, mod, ye):
    e, n = pos.shape
    cap, d = ye.shape[1], ye.shape[2]
    ntiles = n // TOK_TILE
    lo = (base // WIN_ALIGN) * WIN_ALIGN
    extra = jnp.max(jnp.maximum((end - lo + WIN_ROWS - 1) // WIN_ROWS - 1, 0), axis=0).astype(jnp.int32)
    tiles_per_seq = SEQ // TOK_TILE
    return pl.pallas_call(
        functools.partial(_combine_kernel, cap=cap, ntiles=ntiles),
        grid_spec=pltpu.PrefetchScalarGridSpec(
            num_scalar_prefetch=3, grid=(ntiles,),
            in_specs=[
                pl.BlockSpec((e, TOK_TILE), lambda i, *_: (0, i)),
                pl.BlockSpec((TOK_TILE, d), lambda i, *_: (i, 0)),
                pl.BlockSpec((None, N_MOD, d), lambda i, *_: (i // tiles_per_seq, 0, 0)),
                pl.BlockSpec(memory_space=pl.ANY),
            ],
            out_specs=pl.BlockSpec((TOK_TILE, d), lambda i, *_: (i, 0)),
            scratch_shapes=[
                pltpu.VMEM((WIN_BUFFERS, e * WIN_ROWS, d), BF16),
                pltpu.VMEM((e * WIN_ROWS, TOK_TILE), BF16),
                pltpu.VMEM((WIN_ROWS, d), BF16),
                pltpu.SemaphoreType.DMA((WIN_BUFFERS,)),
                pltpu.SemaphoreType.DMA(()),
            ]),
        out_shape=jax.ShapeDtypeStruct((n, d), F32),
        compiler_params=_params(("arbitrary",)),
        name="combine",
    )(lo.reshape(-1), end.reshape(-1), extra, pos, x1, mod, ye)


FFN_ROWS = 2048
FFN_SUB = 512
FFN_FT = 512


def _ffn_kernel(idx_ref, idx_next_ref, h_hbm, wg_ref, wu_ref, wd_ref, gate_ref, o_ref,
                xbuf, xb_ref, acc_ref, sem, *, nblk, nf):
    f = pl.program_id(2)
    blk = pl.program_id(0) * pl.num_programs(1) + pl.program_id(1)
    rows_blk = xb_ref.shape[0]
    per_step = rows_blk // nf

    def row_copy(idx, k, j):
        return pltpu.make_async_copy(h_hbm.at[pl.ds(idx[k * per_step + j], 1), :],
                                     xbuf.at[k, pl.ds(j, 1), :], sem)

    def wait_block():
        for k in range(nf):
            pltpu.make_async_copy(h_hbm.at[pl.ds(0, per_step), :], xbuf.at[k], sem).wait()

    @pl.when((blk == 0) & (f == 0))
    def _():
        for k in range(nf):
            def issue(j, carry):
                row_copy(idx_ref, k, j).start()
                return carry
            lax.fori_loop(0, per_step, issue, 0)

    @pl.when(f == 0)
    def _():
        wait_block()
        for k in range(nf):
            xb_ref[k * per_step:(k + 1) * per_step, :] = xbuf[k].astype(BF16)
        acc_ref[...] = jnp.zeros_like(acc_ref)

    for j in range(per_step):
        row_copy(idx_next_ref, f, j).start()

    wg = wg_ref[...].astype(BF16)
    wu = wu_ref[...].astype(BF16)
    wd = wd_ref[...].astype(BF16)
    sub = min(FFN_SUB, rows_blk)
    for t in range(rows_blk // sub):
        rows = slice(t * sub, (t + 1) * sub)
        xt = xb_ref[rows, :]
        a = _dot(xt, wg)
        b = _dot(xt, wu)
        acc_ref[rows, :] += _dot((_silu(a) * b).astype(BF16), wd)

    @pl.when(f == nf - 1)
    def _():
        o_ref[...] = (acc_ref[...] * gate_ref[...]).astype(o_ref.dtype)

    @pl.when((blk == nblk - 1) & (f == nf - 1))
    def _():
        wait_block()


def _expert_ffn(idx, h, w_gate, w_up, w_down, gates, layer):
    e, cap = gates.shape[0], gates.shape[1]
    d = h.shape[1]
    ff = w_gate.shape[3]
    rb = min(cap, FFN_ROWS)
    nrb = cap // rb
    nblk = e * nrb
    nf = ff // FFN_FT
    return pl.pallas_call(
        functools.partial(_ffn_kernel, nblk=nblk, nf=nf),
        grid=(e, nrb, nf),
        in_specs=[
            pl.BlockSpec((rb,), lambda i, r, f: (i * nrb + r,), memory_space=pltpu.SMEM),
            pl.BlockSpec((rb,), lambda i, r, f: (jnp.minimum(i * nrb + r + 1, nblk - 1),),
                         memory_space=pltpu.SMEM),
            pl.BlockSpec(memory_space=pl.ANY),
            pl.BlockSpec((None, None, d, FFN_FT), lambda i, r, f: (layer, i, 0, f)),
            pl.BlockSpec((None, None, d, FFN_FT), lambda i, r, f: (layer, i, 0, f)),
            pl.BlockSpec((None, None, FFN_FT, d), lambda i, r, f: (layer, i, f, 0)),
            pl.BlockSpec((None, rb, 1), lambda i, r, f: (i, r, 0)),
        ],
        out_specs=pl.BlockSpec((None, rb, d), lambda i, r, f: (i, r, 0)),
        out_shape=jax.ShapeDtypeStruct((e, cap, d), BF16),
        scratch_shapes=[
            pltpu.VMEM((nf, rb // nf, d), F32),
            pltpu.VMEM((rb, d), BF16),
            pltpu.VMEM((rb, d), F32),
            pltpu.SemaphoreType.DMA(()),
        ],
        compiler_params=_params(("arbitrary", "arbitrary", "arbitrary")),
        name="expert_ffn",
    )(idx, idx, h, w_gate, w_up, w_down, gates)


def _moe(h2, logits, x1, mod, wg, wu, wd, layer):
    n, d = h2.shape
    cap = max(1, EC_FACTOR * n // N_EXPERTS)
    lt = logits.reshape(N_EXPERTS, n // LANES, LANES)
    aff, pos, rank = _route(lt, cap)
    aff = aff.reshape(N_EXPERTS, n)
    pos = pos.reshape(N_EXPERTS, n)
    base = rank.reshape(N_EXPERTS, n)[:, ::TOK_TILE]
    end = jnp.concatenate([base[:, 1:], jnp.full((N_EXPERTS, 1), cap, jnp.int32)], axis=1)
    idx, gates = _invert(base, end, pos, aff, cap)
    ye = _expert_ffn(idx, h2, wg, wu, wd, gates[..., None], layer)
    return _combine(base, end, pos, x1, mod, ye)


def _trunk(x, mod, p):
    b = x.shape[0]
    n = b * SEQ
    xf = x.reshape(n, D_MODEL)
    for i in range(2):
        g1 = p["norm1_g"][i].reshape(1, D_MODEL)
        g2 = p["norm2_g"][i].reshape(1, D_MODEL)
        if i == 0:
            qkv = _norm_qkv(xf, mod[i], g1, p["w_qkv"])
            m = _attention(qkv.reshape(b, SEQ, 3 * D_MODEL), p["q_gain"], p["k_gain"], p["bias"])
            w_mix = p["w_o"]
        else:
            v = _fnet_in(xf, mod[i], g1, p["w_in"], p["cs"])
            m = _seq_dft(v.reshape(b, SEQ, 2 * D_MODEL), p["ct"], p["st_neg"])
            w_mix = p["w_out"]
        x1, h2, logits = _mix_out(m.reshape(n, D_MODEL), w_mix, xf, mod[i], g2, p["w_router"][i])
        xf = _moe(h2, logits, x1, mod[i], p["w_gate"], p["w_up"], p["w_down"], i)
    return xf.reshape(b, SEQ, D_MODEL)


def _split_router(w):
    wt = jnp.swapaxes(w, 1, 2)
    hi = wt.astype(BF16)
    lo = (wt - hi.astype(F32)).astype(BF16)
    return jnp.concatenate([hi, lo], axis=1)


def kernel(x_prompt, x_sample, c_prompt, c_sample, norm1_g, norm2_g, ada_w, ada_b, na_w_qkv, na_q_g, na_k_g, na_rpb, na_w_o, fn_w_in, fn_w_out, moe_w_router, moe_w_gate, moe_w_up, moe_w_down):
    cs, ct, st_neg = _dft_tables()
    tile2 = lambda g: jnp.tile(g.reshape(1, HEAD_DIM), (1, HEADS_PER_BLOCK))
    p = {
        "norm1_g": norm1_g, "norm2_g": norm2_g, "ada_w": ada_w, "ada_b": ada_b,
        "w_qkv": na_w_qkv[0].astype(BF16),
        "q_gain": tile2(na_q_g[0]) * (HEAD_DIM ** -0.5),
        "k_gain": tile2(na_k_g[0]),
        "bias": _attn_bias_tables(na_rpb[0]),
        "w_o": na_w_o[0].astype(BF16),
        "w_in": fn_w_in[0].astype(BF16),
        "w_out": fn_w_out[0].astype(BF16),
        "cs": cs, "ct": ct, "st_neg": st_neg,
        "w_router": _split_router(moe_w_router),
        "w_gate": moe_w_gate, "w_up": moe_w_up, "w_down": moe_w_down,
    }
    nb = x_prompt.shape[0]
    mod = _ada(jnp.concatenate([c_prompt, c_sample], axis=0), ada_w, ada_b)
    return (_trunk(x_prompt, mod[:, :nb], p), _trunk(x_sample, mod[:, nb:], p))
```
